```python
import math
import jax
import jax.numpy as jnp
from jax import lax
import numpy as np

D_MODEL = 4096
BATCH = 2
SEQ = 4096
DEPTH = 1
DEC_BATCH = 128
DEC_SEQ = 4
PAST_LEN = 2048
PAGE_SIZE = 128

DA_HD = D_MODEL // 64
DA_HEADS = D_MODEL // 256
DA_QK = 2 * DA_HD
DA_DV = 2 * DA_HD
DA_WIDTH = DA_HEADS * DA_DV
Q_BLOCK = 128
ALIBI_MAX = 8.0
M_HEADS = 8
M_DV = (D_MODEL // 2) // M_HEADS
M_DK = M_DV // 2
M_WIDTH = M_HEADS * M_DV
M_CONV_CH = 2 * M_HEADS * M_DK
CONV_W = 4
M_CHUNK = 128
N_GROUPS = 8
EXP_PER_GROUP = 8
N_EXPERTS = N_GROUPS * EXP_PER_GROUP
TOP_K = 2
FF_EXPERT = D_MODEL // 4
MOE_BLOCK = 128
EPS = 1e-6
SPLITS = (DA_HEADS * DA_QK,
          2 * DA_HEADS * DA_QK,
          2 * DA_HEADS * DA_QK + DA_WIDTH,
          2 * DA_HEADS * DA_QK + DA_WIDTH + M_CONV_CH,
          2 * DA_HEADS * DA_QK + DA_WIDTH + M_CONV_CH + M_WIDTH,
          2 * DA_HEADS * DA_QK + DA_WIDTH + M_CONV_CH + 2 * M_WIDTH)
MIX_IN = SPLITS[-1] + 2 * M_HEADS

kernel_name = 'hymba_mlstm_diffattn_hmoe_step'


def rms_norm(x, g):
    xf = x.astype(jnp.float32)
    y = xf * lax.rsqrt(jnp.mean(xf * xf, axis=-1, keepdims=True) + EPS)
    return (y * g.astype(jnp.float32)).astype(x.dtype)


def alibi_slopes(n):
    return jnp.asarray(np.array([2.0 ** (-ALIBI_MAX * (h + 1) / n) for h in range(n)], np.float32))


def causal_conv(u, buf, w, b):
    T = u.shape[1]
    xp = jnp.concatenate([buf.astype(u.dtype), u], axis=1)
    out = b + sum(xp[:, j:j + T] * w[j] for j in range(CONV_W))
    return out, xp[:, T:]


def diff_attention(q, k, v, q_pos, k_pos, slopes, lam):
    B, Tq = q.shape[:2]
    blk = math.gcd(Tq, Q_BLOCK)
    nb = Tq // blk
    qb = q.reshape(B, nb, blk, DA_HEADS, 2, DA_HD).transpose(1, 0, 2, 3, 4, 5)
    pb = q_pos.reshape(nb, blk)

    def one(args):
        qi, pi = args
        s = jnp.einsum('bqhcd,bkhcd->bhcqk', qi, k, preferred_element_type=jnp.float32)
        dist = pi[:, None] - k_pos[None, :]
        s = jnp.where(dist >= 0, s - slopes[None, :, None, None, None] * dist.astype(jnp.float32), -jnp.inf)
        p = jax.nn.softmax(s, axis=-1)
        a = p[:, :, 0] - lam * p[:, :, 1]
        return jnp.einsum('bhqk,bkhd->bqhd', a.astype(v.dtype), v)

    out = lax.map(one, (qb, pb))
    return out.transpose(1, 0, 2, 3, 4).reshape(B, Tq, DA_HEADS, DA_DV)


def mlstm_chunkwise(q, k, v, ig, lf, C0, n0, m0):
    B, T, H, DK = q.shape
    L = math.gcd(T, M_CHUNK)
    nc = T // L

    def chunks(a):
        a = a.astype(jnp.float32).reshape((B, nc, L) + a.shape[2:])
        return jnp.moveaxis(a, (1, 3), (0, 2))

    causal = jnp.tril(jnp.ones((L, L), bool))

    def step(carry, xs):
        C, n, m = carry
        qc, kc, vc, ic, fc = xs
        b = jnp.cumsum(fc, axis=-1)
        D = jnp.where(causal, b[..., :, None] - b[..., None, :] + ic[..., None, :], -jnp.inf)
        inter = b + m[..., None]
        mt = jnp.maximum(inter, jnp.max(D, axis=-1))
        W = jnp.einsum('bhtd,bhsd->bhts', qc, kc) * jnp.exp(D - mt[..., None])
        e_in = jnp.exp(inter - mt)
        num = e_in[..., None] * jnp.einsum('bhtd,bhde->bhte', qc, C) + jnp.einsum('bhts,bhse->bhte', W, vc)
        den = e_in * jnp.einsum('bhtd,bhd->bht', qc, n) + jnp.sum(W, axis=-1)
        h = num / jnp.maximum(jnp.abs(den), jnp.exp(-mt))[..., None]
        bL = b[..., -1]
        g = bL[..., None] - b + ic
        m_new = jnp.maximum(bL + m, jnp.max(g, axis=-1))
        e_old = jnp.exp(bL + m - m_new)
        wg = jnp.exp(g - m_new[..., None])
        C_new = e_old[..., None, None] * C + jnp.einsum('bhs,bhsd,bhse->bhde', wg, kc, vc)
        n_new = e_old[..., None] * n + jnp.einsum('bhs,bhsd->bhd', wg, kc)
        return (C_new, n_new, m_new), h

    init = (C0.astype(jnp.float32), n0.astype(jnp.float32), m0.astype(jnp.float32))
    xs = (chunks(q) * (DK ** -0.5), chunks(k), chunks(v), chunks(ig), chunks(lf))
    (C, n, m), h = lax.scan(step, init, xs)
    h = jnp.moveaxis(h, (0, 2), (1, 3)).reshape(B, T, H, v.shape[-1])
    return h, C, n, m


def hier_moe(x, w_group, b_group, w_router, b_router, w_gate, w_up, w_down):
    N = x.shape[0]
    f32 = jnp.float32
    lg = jnp.einsum('nd,dg->ng', x, w_group, preferred_element_type=f32) + b_group.astype(f32)
    gval, gidx = lax.top_k(jax.nn.softmax(lg, axis=-1), 1)
    le = jnp.einsum('nd,de->ne', x, w_router, preferred_element_type=f32) + b_router.astype(f32)
    le = le.reshape(N, N_GROUPS, EXP_PER_GROUP)[jnp.arange(N), gidx[:, 0]]
    ev, eidx = lax.top_k(le, TOP_K)
    wts = (gval * jax.nn.softmax(ev, axis=-1)).reshape(-1)
    eid = (gidx * EXP_PER_GROUP + eidx).reshape(-1).astype(jnp.int32)
    tok = jnp.repeat(jnp.arange(N, dtype=jnp.int32), TOP_K)
    n_assign = N * TOP_K
    order = jnp.argsort(eid)
    e_s, t_s, w_s = eid[order], tok[order], wts[order]
    counts = jnp.bincount(eid, length=N_EXPERTS)
    starts = jnp.cumsum(counts) - counts
    padded = (counts + MOE_BLOCK - 1) // MOE_BLOCK * MOE_BLOCK
    pend = jnp.cumsum(padded)
    pstart = pend - padded
    dest = pstart[e_s] + jnp.arange(n_assign, dtype=jnp.int32) - starts[e_s]
    n_blocks = (n_assign + N_EXPERTS * (MOE_BLOCK - 1) + MOE_BLOCK - 1) // MOE_BLOCK
    rows = n_blocks * MOE_BLOCK
    xb = jnp.zeros((rows, x.shape[1]), x.dtype).at[dest].set(x[t_s])
    wb = jnp.zeros((rows,), f32).at[dest].set(w_s)
    tb = jnp.zeros((rows,), jnp.int32).at[dest].set(t_s)
    block_e = jnp.clip(jnp.searchsorted(pend, jnp.arange(n_blocks, dtype=jnp.int32) * MOE_BLOCK, side='right'), 0, N_EXPERTS - 1)

    def expert_block(args):
        xi, e = args
        hdn = jax.nn.silu(xi @ w_gate[e]) * (xi @ w_up[e])
        return hdn @ w_down[e]

    yb = lax.map(expert_block, (xb.reshape(n_blocks, MOE_BLOCK, -1), block_e))
    y = jax.ops.segment_sum(yb.reshape(rows, -1).astype(f32) * wb[:, None], tb, num_segments=N)
    return y.astype(x.dtype)


def hybrid_layer(x, conv_buf, C0, n0, m0, k_past, v_past, lambda_init, p):
    (norm_mix, w_in, b_gate, conv_w, conv_b, q_norm, k_norm, da_lambda, da_subln,
     m_outnorm, w_out, norm_ffn, w_group, b_group, w_router, b_router, w_gate, w_up, w_down) = p
    B, T, _ = x.shape
    f32 = jnp.float32
    xn = rms_norm(x, norm_mix)
    proj = jnp.einsum('btd,de->bte', xn, w_in)
    da_q, da_k, da_v, m_qk, m_v, m_o, m_g = jnp.split(proj, SPLITS, axis=-1)
    q = rms_norm(da_q.reshape(B, T, DA_HEADS, 2, DA_HD), q_norm) * (DA_HD ** -0.5)
    k = rms_norm(da_k.reshape(B, T, DA_HEADS, 2, DA_HD), k_norm)
    v = da_v.reshape(B, T, DA_HEADS, DA_DV)
    if k_past is None:
        pos0, k_all, v_all = 0, k, v
    else:
        pos0 = k_past.shape[1]
        k_all = jnp.concatenate([k_past.reshape(B, pos0, DA_HEADS, 2, DA_HD).astype(k.dtype), k], axis=1)
        v_all = jnp.concatenate([v_past.astype(v.dtype), v], axis=1)
    lp = da_lambda.astype(f32)
    lam = jnp.exp(jnp.sum(lp[0] * lp[1])) - jnp.exp(jnp.sum(lp[2] * lp[3])) + lambda_init
    q_pos = pos0 + jnp.arange(T, dtype=jnp.int32)
    k_pos = jnp.arange(pos0 + T, dtype=jnp.int32)
    o_da = diff_attention(q, k_all, v_all, q_pos, k_pos, alibi_slopes(DA_HEADS), lam)
    o_da = (rms_norm(o_da, da_subln) * (1.0 - lambda_init)).astype(x.dtype).reshape(B, T, DA_WIDTH)
    u, conv_new = causal_conv(m_qk, conv_buf, conv_w, conv_b)
    u = jax.nn.silu(u)
    mq, mk = jnp.split(u, 2, axis=-1)
    mq = mq.reshape(B, T, M_HEADS, M_DK)
    mk = mk.reshape(B, T, M_HEADS, M_DK)
    mv = m_v.reshape(B, T, M_HEADS, M_DV)
    gates = m_g.astype(f32) + b_gate.astype(f32)
    ig = gates[..., :M_HEADS]
    lf = jax.nn.log_sigmoid(gates[..., M_HEADS:])
    h, C, n, m = mlstm_chunkwise(mq, mk, mv, ig, lf, C0, n0, m0)
    o_gate = jax.nn.sigmoid(m_o.astype(f32)).reshape(B, T, M_HEADS, M_DV)
    o_m = (o_gate * rms_norm(h, m_outnorm)).astype(x.dtype).reshape(B, T, M_WIDTH)
    mixed = jnp.concatenate([o_da, o_m], axis=-1)
    x = x + jnp.einsum('bte,ed->btd', mixed, w_out)
    xf = rms_norm(x, norm_ffn).reshape(B * T, D_MODEL)
    x = x + hier_moe(xf, w_group, b_group, w_router, b_router, w_gate, w_up, w_down).reshape(B, T, D_MODEL)
    return x, k.reshape(B, T, DA_HEADS, DA_QK), v, C, n, m, conv_new


def setup_inputs(seed: int = 0) -> dict:
    key = jax.random.key(seed)
    ks = jax.random.split(key, 32)
    f32 = jnp.float32

    def nrm(k, shape, scale):
        return jax.random.normal(k, shape, f32) * scale

    n_pages = PAST_LEN // PAGE_SIZE
    used = DEC_BATCH * n_pages
    n_pool = used + max(1, used // 4)
    page_table = jax.random.permutation(ks[0], n_pool)[:used].reshape(DEC_BATCH, n_pages).astype(jnp.int32)
    b_gate = jnp.concatenate([-2.0 + nrm(ks[1], (DEPTH, M_HEADS), 0.1),
                              3.0 + nrm(ks[2], (DEPTH, M_HEADS), 0.1)], axis=-1)
    return {
        'x_prompt': nrm(ks[3], (BATCH, SEQ, D_MODEL), 1.0),
        'x_sample': nrm(ks[4], (DEC_BATCH, DEC_SEQ, D_MODEL), 1.0),
        'cache_k': nrm(ks[5], (DEPTH, n_pool, PAGE_SIZE, DA_HEADS, DA_QK), 1.0),
        'cache_v': nrm(ks[6], (DEPTH, n_pool, PAGE_SIZE, DA_HEADS, DA_DV), 1.0),
        'state_C': nrm(ks[7], (DEPTH, DEC_BATCH, M_HEADS, M_DK, M_DV), 0.1),
        'state_n': nrm(ks[8], (DEPTH, DEC_BATCH, M_HEADS, M_DK), 0.1),
        'state_m': nrm(ks[9], (DEPTH, DEC_BATCH, M_HEADS), 1.0),
        'state_conv': nrm(ks[10], (DEPTH, DEC_BATCH, CONV_W - 1, M_CONV_CH), 1.0),
        'page_table': page_table,
        'norm_mix': 1.0 + nrm(ks[11], (DEPTH, D_MODEL), 0.01),
        'w_in': nrm(ks[12], (DEPTH, D_MODEL, MIX_IN), D_MODEL ** -0.5),
        'b_gate': b_gate,
        'conv_w': nrm(ks[13], (DEPTH, CONV_W, M_CONV_CH), CONV_W ** -0.5),
        'conv_b': nrm(ks[14], (DEPTH, M_CONV_CH), 0.01),
        'q_norm': 1.0 + nrm(ks[15], (DEPTH, DA_HD), 0.01),
        'k_norm': 1.0 + nrm(ks[16], (DEPTH, DA_HD), 0.01),
        'da_lambda': nrm(ks[17], (DEPTH, 4, DA_HD), 0.1),
        'da_subln': 1.0 + nrm(ks[18], (DEPTH, DA_DV), 0.01),
        'm_outnorm': 1.0 + nrm(ks[19], (DEPTH, M_DV), 0.01),
        'w_out': nrm(ks[20], (DEPTH, DA_WIDTH + M_WIDTH, D_MODEL), (DA_WIDTH + M_WIDTH) ** -0.5),
        'norm_ffn': 1.0 + nrm(ks[21], (DEPTH, D_MODEL), 0.01),
        'w_group': nrm(ks[22], (DEPTH, D_MODEL, N_GROUPS), D_MODEL ** -0.5),
        'b_group': nrm(ks[23], (DEPTH, N_GROUPS), 0.01),
        'w_router': nrm(ks[24], (DEPTH, D_MODEL, N_EXPERTS), D_MODEL ** -0.5),
        'b_router': nrm(ks[25], (DEPTH, N_EXPERTS), 0.01),
        'w_gate': nrm(ks[26], (DEPTH, N_EXPERTS, D_MODEL, FF_EXPERT), D_MODEL ** -0.5),
        'w_up': nrm(ks[27], (DEPTH, N_EXPERTS, D_MODEL, FF_EXPERT), D_MODEL ** -0.5),
        'w_down': nrm(ks[28], (DEPTH, N_EXPERTS, FF_EXPERT, D_MODEL), FF_EXPERT ** -0.5),
    }


def reference(x_prompt, x_sample, cache_k, cache_v, state_C, state_n, state_m, state_conv, page_table,
              norm_mix, w_in, b_gate, conv_w, conv_b, q_norm, k_norm, da_lambda, da_subln, m_outnorm,
              w_out, norm_ffn, w_group, b_group, w_router, b_router, w_gate, w_up, w_down):
    f32 = jnp.float32
    yp, ys = x_prompt, x_sample
    bp = x_prompt.shape[0]
    outs = [[] for _ in range(12)]
    for l in range(DEPTH):
        lambda_init = 0.8 - 0.6 * math.exp(-0.3 * l)
        p = (norm_mix[l], w_in[l], b_gate[l], conv_w[l], conv_b[l], q_norm[l], k_norm[l], da_lambda[l],
             da_subln[l], m_outnorm[l], w_out[l], norm_ffn[l], w_group[l], b_group[l], w_router[l],
             b_router[l], w_gate[l], w_up[l], w_down[l])
        yp, kp, vp, Cp, n_p, mp, cvp = hybrid_layer(
            yp, jnp.zeros((bp, CONV_W - 1, M_CONV_CH), yp.dtype), jnp.zeros((bp, M_HEADS, M_DK, M_DV), f32),
            jnp.zeros((bp, M_HEADS, M_DK), f32), jnp.zeros((bp, M_HEADS), f32), None, None, lambda_init, p)
        k_past = cache_k[l][page_table].reshape(DEC_BATCH, -1, DA_HEADS, DA_QK)
        v_past = cache_v[l][page_table].reshape(DEC_BATCH, -1, DA_HEADS, DA_DV)
        ys, kss, vss, Cs, n_s, ms, cvs = hybrid_layer(
            ys, state_conv[l], state_C[l], state_n[l], state_m[l], k_past, v_past, lambda_init, p)
        for i, a in enumerate((kp, vp, Cp, n_p, mp, cvp, kss, vss, Cs, n_s, ms, cvs)):
            outs[i].append(a)
    (k_prompt, v_prompt, C_prompt, n_prompt, m_prompt, conv_prompt,
     k_sample, v_sample, C_sample, n_sample, m_sample, conv_sample) = [jnp.stack(o) for o in outs]
    return (yp, ys, k_prompt, v_prompt, C_prompt, n_prompt, m_prompt, conv_prompt,
            k_sample, v_sample, C_sample, n_sample, m_sample, conv_sample)
```

```python
import functools
import math

import numpy as np
import jax
import jax.numpy as jnp
from jax import lax
from jax.experimental import pallas as pl
from jax.experimental.pallas import tpu as pltpu

F32 = jnp.float32
BF16 = jnp.bfloat16

EPS = 1e-6
ALIBI_MAX = 8.0
CONV_W = 4
TOP_K = 2
M_CHUNK = 128
NEG = -1e30

LANES = 128
SUBLANES = 8
VMEM_BYTES_V7X = 64 * 1024 * 1024
VMEM_LIMIT = VMEM_BYTES_V7X - 8 * 1024 * 1024


def _row_tile(n, cap):
    t = cap
    while t > SUBLANES and n % t:
        t //= 2
    assert n % t == 0, (n, t)
    return t


def _params(sem, vmem=None):
    return pltpu.CompilerParams(dimension_semantics=sem, vmem_limit_bytes=vmem or VMEM_LIMIT)


def _proj_kernel(x_ref, g_ref, w_ref, wg_ref, o_ref, og_ref, xn_ref):
    @pl.when(pl.program_id(1) == 0)
    def _():
        x = x_ref[...]
        ms = jnp.mean(x * x, axis=-1, keepdims=True)
        xn = ((x * lax.rsqrt(ms + EPS)) * g_ref[...]).astype(BF16)
        xn_ref[...] = xn
        og_ref[...] = jnp.dot(xn, wg_ref[...], preferred_element_type=F32)

    o_ref[0] = jnp.dot(xn_ref[...], w_ref[...], preferred_element_type=F32)


def _input_projection(x, g, w, wg, region):
    n, d = x.shape
    e = w.shape[1]
    tm = _row_tile(n, 512)
    tn = min(1024, region)
    per = region // tn
    return pl.pallas_call(
        _proj_kernel,
        grid=(n // tm, e // tn),
        in_specs=[
            pl.BlockSpec((tm, d), lambda i, j: (i, 0)),
            pl.BlockSpec((1, d), lambda i, j: (0, 0)),
            pl.BlockSpec((d, tn), lambda i, j: (0, j)),
            pl.BlockSpec((d, LANES), lambda i, j: (0, 0)),
        ],
        out_specs=[
            pl.BlockSpec((1, tm, tn), lambda i, j: (j // per, i, j % per)),
            pl.BlockSpec((tm, LANES), lambda i, j: (i, 0)),
        ],
        out_shape=[
            jax.ShapeDtypeStruct((e // region, n, region), F32),
            jax.ShapeDtypeStruct((n, LANES), F32),
        ],
        scratch_shapes=[pltpu.VMEM((tm, d), BF16)],
        compiler_params=_params(("arbitrary", "arbitrary")),
        name="input_projection",
    )(x, g.reshape(1, d), w, wg)


def _qkprep_kernel(q_ref, k_ref, v_ref, gq_ref, gk_ref, grp_ref, qo_ref, ko_ref, kbo_ref, vbo_ref,
                   *, heads, half, q_scale):
    grp = grp_ref[...]

    def half_norm(x, g):
        sq = x * x
        hi = sq.astype(BF16)
        lo = (sq - hi.astype(F32)).astype(BF16)
        s = jnp.dot(hi, grp, preferred_element_type=F32) + jnp.dot(lo, grp, preferred_element_type=F32)
        return (x * lax.rsqrt(s * (1.0 / half) + EPS)) * g

    for h in range(heads):
        sl = slice(h * LANES, (h + 1) * LANES)
        qn = half_norm(q_ref[0, :, sl], gq_ref[...])
        qo_ref[:, sl] = (qn * q_scale).astype(BF16)
        kn = half_norm(k_ref[0, :, sl], gk_ref[...])
        ko_ref[:, sl] = kn
        kbo_ref[:, sl] = kn.astype(BF16)
    vbo_ref[...] = v_ref[0].astype(BF16)


def _qk_prep(proj, q_norm, k_norm):
    _, n, w = proj.shape
    heads = w // LANES
    half = q_norm.shape[0]
    assert 2 * half == LANES
    tm = _row_tile(n, 256)
    lane = np.arange(LANES)
    grp = jnp.asarray((lane[:, None] // half == lane[None, :] // half).astype(np.float32), BF16)
    gq = jnp.concatenate([q_norm, q_norm]).reshape(1, LANES)
    gk = jnp.concatenate([k_norm, k_norm]).reshape(1, LANES)
    row = lambda r: pl.BlockSpec((1, tm, w), lambda i: (r, i, 0))
    full = lambda shape: pl.BlockSpec(shape, lambda i: (0, 0))
    out = pl.BlockSpec((tm, w), lambda i: (i, 0))
    return pl.pallas_call(
        functools.partial(_qkprep_kernel, heads=heads, half=half, q_scale=half ** -0.5),
        grid=(n // tm,),
        in_specs=[row(0), row(1), row(2), full((1, LANES)), full((1, LANES)), full((LANES, LANES))],
        out_specs=[out, out, out, out],
        out_shape=[
            jax.ShapeDtypeStruct((n, w), BF16),
            jax.ShapeDtypeStruct((n, w), F32),
            jax.ShapeDtypeStruct((n, w), BF16),
            jax.ShapeDtypeStruct((n, w), BF16),
        ],
        compiler_params=_params(("arbitrary",)),
        name="qk_prep",
    )(proj, proj, proj, gq, gk, grp)


def _alibi_slopes(n):
    return np.array([2.0 ** (-ALIBI_MAX * (h + 1) / n) for h in range(n)], np.float32)


def _subln(o, g, out_scale):
    ms = jnp.mean(o * o, axis=-1, keepdims=True)
    return ((o * lax.rsqrt(ms + EPS)) * g) * out_scale


def _attn_kernel(slopes_ref, lam_ref, q_ref, k_ref, v_ref, gs_ref, o_ref,
                 m0_ref, l0_ref, a0_ref, m1_ref, l1_ref, a1_ref, *, tq, half, out_scale):
    h = pl.program_id(1)
    i = pl.program_id(2)
    slope = slopes_ref[h]
    q = q_ref[0]
    lane = lax.broadcasted_iota(jnp.int32, (1, LANES), 1)
    zero = jnp.zeros_like(q)
    qs = (jnp.where(lane < half, q, zero), jnp.where(lane >= half, q, zero))
    states = ((m0_ref, l0_ref, a0_ref), (m1_ref, l1_ref, a1_ref))
    for m_ref, l_ref, a_ref in states:
        m_ref[...] = jnp.full(m_ref.shape, NEG, F32)
        l_ref[...] = jnp.zeros(l_ref.shape, F32)
        a_ref[...] = jnp.zeros(a_ref.shape, F32)
    rel = (lax.broadcasted_iota(jnp.int32, (tq, tq), 0) - lax.broadcasted_iota(jnp.int32, (tq, tq), 1))
    rel_f = rel.astype(F32)

    def tile(j, masked):
        start = pl.multiple_of(j * tq, tq)
        ks = k_ref[0, pl.ds(start, tq), :]
        vs = v_ref[0, pl.ds(start, tq), :]
        base = ((i - j) * tq).astype(F32)
        bias = slope * (rel_f + base)
        for qc, (m_ref, l_ref, a_ref) in zip(qs, states):
            s = lax.dot_general(qc, ks, (((1,), (1,)), ((), ())), preferred_element_type=F32) - bias
            if masked:
                s = jnp.where(rel >= 0, s, NEG)
            m_old = m_ref[...]
            m_new = jnp.maximum(m_old, jnp.max(s, axis=-1, keepdims=True))
            alpha = jnp.exp(m_old - m_new)
            p = jnp.exp(s - m_new)
            l_ref[...] = alpha * l_ref[...] + jnp.sum(p, axis=-1, keepdims=True)
            a_ref[...] = alpha * a_ref[...] + jnp.dot(p.astype(BF16), vs, preferred_element_type=F32)
            m_ref[...] = m_new

    def body(j, carry):
        tile(j, False)
        return carry

    lax.fori_loop(0, i, body, 0)
    tile(i, True)
    o = a0_ref[...] / l0_ref[...] - lam_ref[0] * (a1_ref[...] / l1_ref[...])
    o_ref[0] = _subln(o, gs_ref[...], out_scale).astype(o_ref.dtype)


def _prompt_attention(q, k, v, lam, subln, out_scale):
    b, t, w = q.shape
    heads = w // LANES
    tq = _row_tile(t, 512)
    slopes = jnp.asarray(_alibi_slopes(heads))
    kv = pl.BlockSpec((1, t, LANES), lambda bi, hi, i, *_: (bi, 0, hi))
    qo = pl.BlockSpec((1, tq, LANES), lambda bi, hi, i, *_: (bi, i, hi))
    return pl.pallas_call(
        functools.partial(_attn_kernel, tq=tq, half=LANES // 2, out_scale=out_scale),
        grid_spec=pltpu.PrefetchScalarGridSpec(
            num_scalar_prefetch=2,
            grid=(b, heads, t // tq),
            in_specs=[qo, kv, kv, pl.BlockSpec((1, LANES), lambda bi, hi, i, *_: (0, 0))],
            out_specs=qo,
            scratch_shapes=[pltpu.VMEM((tq, 1), F32), pltpu.VMEM((tq, 1), F32), pltpu.VMEM((tq, LANES), F32)] * 2,
        ),
        out_shape=jax.ShapeDtypeStruct((b, t, w), BF16),
        compiler_params=_params(("arbitrary", "arbitrary", "arbitrary")),
        name="prompt_diff_attention",
    )(slopes, lam.reshape(1), q, k, v, subln.reshape(1, LANES))


QROWS = 16


def _paged_attn_kernel(pt_ref, lam_ref, *refs, heads, pages, page, t_new, past_len, out_scale):
    kp_refs = refs[:pages]
    vp_refs = refs[pages:2 * pages]
    (q_ref, kn_ref, vn_ref, slope_ref, qi_ref, gs_ref, o_ref, m_ref, l_ref, a_ref) = refs[2 * pages:]
    g = pl.program_id(1)
    rows = heads * QROWS
    slope_c = slope_ref[...]
    qi_c = qi_ref[...]

    def head_slice(h):
        return slice(h * LANES, (h + 1) * LANES)

    def scores(k_tiles):
        per_head = []
        for h in range(heads):
            qh = q_ref[0, h]
            parts = [lax.dot_general(qh, kt[:, head_slice(h)], (((1,), (1,)), ((), ())),
                                     preferred_element_type=F32) for kt in k_tiles]
            per_head.append(parts[0] if len(parts) == 1 else jnp.concatenate(parts, axis=1))
        return jnp.concatenate(per_head, axis=0)

    def update(s, v_tiles, tks, first):
        m_old = jnp.full((rows, 1), NEG, F32) if first else m_ref[...]
        m_new = jnp.maximum(m_old, jnp.max(s, axis=-1, keepdims=True))
        p = jnp.exp(s - m_new)
        psum = jnp.sum(p, axis=-1, keepdims=True)
        pb = p.astype(BF16)
        alpha = jnp.exp(m_old - m_new)
        for h in range(heads):
            r = slice(h * QROWS, (h + 1) * QROWS)
            pv = None
            off = 0
            for vt, tk in zip(v_tiles, tks):
                d = jnp.dot(pb[r, off:off + tk], vt[:, head_slice(h)], preferred_element_type=F32)
                pv = d if pv is None else pv + d
                off += tk
            a_ref[r, :] = pv if first else alpha[r] * a_ref[r, :] + pv
        l_ref[...] = psum if first else alpha * l_ref[...] + psum
        m_ref[...] = m_new

    @pl.when(g == 0)
    def _():
        kn = kn_ref[0]
        tk = kn.shape[0]
        s = scores([kn])
        col = lax.broadcasted_iota(jnp.int32, (1, tk), 1).astype(F32)
        dist = qi_c - col
        s = jnp.where((dist >= 0) & (col < t_new), s - slope_c * dist, NEG)
        update(s, [vn_ref[0]], [tk], True)

    k_tiles = [r[0].astype(BF16) for r in kp_refs]
    v_tiles = [r[0].astype(BF16) for r in vp_refs]
    s = scores(k_tiles)
    tk = pages * page
    kpos = (g * tk + lax.broadcasted_iota(jnp.int32, (1, tk), 1)).astype(F32)
    s = s - slope_c * ((qi_c + float(past_len)) - kpos)
    update(s, v_tiles, [page] * pages, False)

    @pl.when(g == pl.num_programs(1) - 1)
    def _():
        lam = lam_ref[0]
        for h in range(heads):
            r = slice(h * QROWS, (h + 1) * QROWS)
            a = a_ref[r, :] / l_ref[r, :]
            o = a[0:SUBLANES] - lam * a[SUBLANES:QROWS]
            o_ref[0, :, head_slice(h)] = _subln(o, gs_ref[...], out_scale).astype(o_ref.dtype)


def _paged_attention(q, k_new, v_new, cache_k, cache_v, page_table, lam, subln, out_scale):
    b, tq, w = q.shape
    heads = w // LANES
    half = LANES // 2
    n_pages = page_table.shape[1]
    page = cache_k.shape[1]
    pages = math.gcd(n_pages, 4)
    assert tq <= SUBLANES
    qh = q.reshape(b, tq, heads, LANES).transpose(0, 2, 1, 3)
    lane = jnp.arange(LANES)
    pad = ((0, 0), (0, 0), (0, SUBLANES - tq), (0, 0))
    q16 = jnp.concatenate([jnp.pad(jnp.where(lane < half, qh, 0), pad),
                           jnp.pad(jnp.where(lane >= half, qh, 0), pad)], axis=2)
    kn = jnp.pad(k_new, ((0, 0), (0, QROWS - tq), (0, 0)))
    vn = jnp.pad(v_new, ((0, 0), (0, QROWS - tq), (0, 0)))
    slopes = _alibi_slopes(heads)
    slope_col = jnp.asarray(np.repeat(slopes, QROWS).reshape(-1, 1))
    qi_col = jnp.asarray(np.tile(np.arange(QROWS) % SUBLANES, heads).astype(np.float32).reshape(-1, 1))
    rows = heads * QROWS

    def page_spec(p):
        return pl.BlockSpec((1, page, w), lambda bi, g, pt, lm: (pt[bi * n_pages + g * pages + p], 0, 0))

    per_b = lambda shape: pl.BlockSpec(shape, lambda bi, g, pt, lm: (bi,) + (0,) * (len(shape) - 1))
    const = lambda shape: pl.BlockSpec(shape, lambda bi, g, pt, lm: (0,) * len(shape))
    kernel = functools.partial(_paged_attn_kernel, heads=heads, pages=pages, page=page, t_new=tq,
                               past_len=n_pages * page, out_scale=out_scale)
    out = pl.pallas_call(
        kernel,
        grid_spec=pltpu.PrefetchScalarGridSpec(
            num_scalar_prefetch=2,
            grid=(b, n_pages // pages),
            in_specs=[page_spec(p) for p in range(pages)] * 2 + [
                per_b((1, heads, QROWS, LANES)), per_b((1, QROWS, w)), per_b((1, QROWS, w)),
                const((rows, 1)), const((rows, 1)), const((1, LANES))],
            out_specs=per_b((1, SUBLANES, w)),
            scratch_shapes=[pltpu.VMEM((rows, 1), F32), pltpu.VMEM((rows, 1), F32),
                            pltpu.VMEM((rows, LANES), F32)],
        ),
        out_shape=jax.ShapeDtypeStruct((b, SUBLANES, w), F32),
        compiler_params=_params(("arbitrary", "arbitrary")),
        name="paged_diff_attention",
    )(page_table.reshape(-1), lam.reshape(1), *([cache_k] * pages), *([cache_v] * pages),
      q16, kn, vn, slope_col, qi_col, subln.reshape(1, LANES))
    return out[:, :tq].astype(BF16)


def _conv_kernel(x_ref, st_ref, w_ref, b_ref, qo_ref, ko_ref, buf_ref, *, tt, q_scale, carry):
    @pl.when(pl.program_id(1) == 0)
    def _():
        buf_ref[0:SUBLANES, :] = st_ref[0]

    x = x_ref[0]
    buf_ref[SUBLANES:SUBLANES + tt, :] = x
    acc = b_ref[...] + w_ref[CONV_W - 1:CONV_W, :] * x
    for k in range(1, CONV_W):
        acc = acc + w_ref[CONV_W - 1 - k:CONV_W - k, :] * buf_ref[SUBLANES - k:SUBLANES - k + tt, :]
    u = acc * jax.nn.sigmoid(acc)
    c = u.shape[1] // 2
    qo_ref[0] = (u[:, :c] * q_scale).astype(qo_ref.dtype)
    ko_ref[0] = u[:, c:].astype(ko_ref.dtype)
    if carry:
        buf_ref[0:SUBLANES, :] = buf_ref[tt:tt + SUBLANES, :]


def _conv_silu(x, region, n_batch, t, state8, conv_w, conv_b, q_scale):
    c = x.shape[2]
    tt = _row_tile(t, 256)
    nt = t // tt
    w8 = jnp.pad(conv_w, ((0, SUBLANES - CONV_W), (0, 0)))
    out = pl.BlockSpec((1, tt, c // 2), lambda bi, i: (bi, i, 0))
    return pl.pallas_call(
        functools.partial(_conv_kernel, tt=tt, q_scale=q_scale, carry=nt > 1),
        grid=(n_batch, nt),
        in_specs=[
            pl.BlockSpec((1, tt, c), lambda bi, i: (region, bi * nt + i, 0)),
            pl.BlockSpec((1, SUBLANES, c), lambda bi, i: (bi, 0, 0)),
            pl.BlockSpec((SUBLANES, c), lambda bi, i: (0, 0)),
            pl.BlockSpec((1, c), lambda bi, i: (0, 0)),
        ],
        out_specs=[out, out],
        out_shape=[jax.ShapeDtypeStruct((n_batch, t, c // 2), BF16)] * 2,
        scratch_shapes=[pltpu.VMEM((tt + SUBLANES, c), F32)],
        compiler_params=_params(("arbitrary", "arbitrary")),
        name="mlstm_conv_silu",
    )(x, state8, w8, conv_b.reshape(1, c))


def _mlstm_kernel(q_ref, k_ref, v_ref, og_ref, g_ref, bg_ref, on_ref, c0_ref, n0_ref, m0_ref,
                  o_ref, c_ref, n_ref, m_ref, cs_ref, ns_ref, ms_ref, *, heads, dk, dv, rows, t_valid):
    L = M_CHUNK
    ci = pl.program_id(1)

    @pl.when(ci == 0)
    def _():
        cs_ref[...] = c0_ref[0]
        ns_ref[...] = n0_ref[0]
        ms_ref[...] = m0_ref[0]

    def pad_rows(a):
        if rows == L:
            return a
        return jnp.concatenate([a, jnp.zeros((L - rows, a.shape[1]), a.dtype)], axis=0)

    gates = pad_rows(g_ref[0] + bg_ref[...])
    lf_all = jnp.minimum(gates, 0.0) - jnp.log1p(jnp.exp(-jnp.abs(gates)))
    q_all = pad_rows(q_ref[0]).astype(BF16)
    k_all = pad_rows(k_ref[0]).astype(BF16)
    v_all = pad_rows(v_ref[0]).astype(BF16)
    r_i = lax.broadcasted_iota(jnp.int32, (L, L), 0)
    c_i = lax.broadcasted_iota(jnp.int32, (L, L), 1)
    eye = r_i == c_i
    tril = c_i <= r_i
    valid = lax.broadcasted_iota(jnp.int32, (L, 1), 0) < t_valid

    def to_row(col):
        return jnp.sum(jnp.where(eye, col, 0.0), axis=0, keepdims=True)

    for h in range(heads):
        ig_c = jnp.where(valid, gates[:, h:h + 1], NEG)
        lf_c = jnp.where(valid, lf_all[:, heads + h:heads + h + 1], 0.0)
        ig_r = to_row(ig_c)
        lf_r = to_row(lf_c)
        b_c = jnp.sum(jnp.where(tril, lf_r, 0.0), axis=1, keepdims=True)
        b_r = to_row(b_c)
        m_prev = ms_ref[h:h + 1, 0:1]
        dmat = jnp.where(tril, (b_c - b_r) + ig_r, NEG)
        inter = b_c + m_prev
        mt = jnp.maximum(inter, jnp.max(dmat, axis=1, keepdims=True))
        qh = q_all[:, h * dk:(h + 1) * dk]
        kh = k_all[:, h * dk:(h + 1) * dk]
        vh = v_all[:, h * dv:(h + 1) * dv]
        c_old = cs_ref[h]
        n_old = ns_ref[h:h + 1, :]
        s = lax.dot_general(qh, kh, (((1,), (1,)), ((), ())), preferred_element_type=F32)
        wm = s * jnp.exp(dmat - mt)
        e_in = jnp.exp(inter - mt)
        num = e_in * jnp.dot(qh, c_old.astype(BF16), preferred_element_type=F32) \
            + jnp.dot(wm.astype(BF16), vh, preferred_element_type=F32)
        den = e_in * jnp.sum(qh.astype(F32) * n_old, axis=1, keepdims=True) + jnp.sum(wm, axis=1, keepdims=True)
        hid = num / jnp.maximum(jnp.abs(den), jnp.exp(-mt))
        ms_h = jnp.mean(hid * hid, axis=-1, keepdims=True)
        hn = (hid * lax.rsqrt(ms_h + EPS)) * on_ref[...]
        gate = jax.nn.sigmoid(og_ref[0, :, h * dv:(h + 1) * dv])
        o_ref[0, :, h * dv:(h + 1) * dv] = (gate * hn[:rows]).astype(o_ref.dtype)
        b_last = jnp.sum(lf_r, axis=1, keepdims=True)
        g_c = (b_last - b_c) + ig_c
        g_r = (b_last - b_r) + ig_r
        m_new = jnp.maximum(b_last + m_prev, jnp.max(g_r, axis=1, keepdims=True))
        e_old = jnp.exp((b_last + m_prev) - m_new)
        kw = kh.astype(F32) * jnp.exp(g_c - m_new)
        cs_ref[h] = e_old * c_old + lax.dot_general(kw.astype(BF16), vh, (((0,), (0,)), ((), ())),
                                                    preferred_element_type=F32)
        ns_ref[h:h + 1, :] = e_old * n_old + jnp.sum(kw, axis=0, keepdims=True)
        ms_ref[h:h + 1, :] = jnp.broadcast_to(m_new, (1, LANES))

    @pl.when(ci == pl.num_programs(1) - 1)
    def _():
        c_ref[0] = cs_ref[...]
        n_ref[0] = ns_ref[...]
        m_ref[0] = ms_ref[...]


def _mlstm(q, k, v_src, v_spec, og_src, og_spec, gates, b_gate, outnorm, c0, n0, m0, t_valid):
    b, t, _ = q.shape
    heads, dk, dv = c0.shape[1:]
    rows = min(t, M_CHUNK)
    nc = t // rows
    bg = jnp.pad(b_gate, (0, LANES - b_gate.shape[0])).reshape(1, LANES)
    m0b = jnp.broadcast_to(m0[:, :, None], (b, heads, LANES))
    blk = lambda w: pl.BlockSpec((1, rows, w), lambda bi, ci: (bi, ci, 0))
    per_b = lambda shape: pl.BlockSpec(shape, lambda bi, ci: (bi,) + (0,) * (len(shape) - 1))
    const = lambda shape: pl.BlockSpec(shape, lambda bi, ci: (0,) * len(shape))
    kernel = functools.partial(_mlstm_kernel, heads=heads, dk=dk, dv=dv, rows=rows, t_valid=t_valid)
    o, c, n, m = pl.pallas_call(
        kernel,
        grid=(b, nc),
        in_specs=[blk(heads * dk), blk(heads * dk), v_spec(rows, nc), og_spec(rows, nc), blk(LANES),
                  const((1, LANES)), const((1, dv)), per_b((1, heads, dk, dv)), per_b((1, heads, dk)),
                  per_b((1, heads, LANES))],
        out_specs=[blk(heads * dv), per_b((1, heads, dk, dv)), per_b((1, heads, dk)), per_b((1, heads, LANES))],
        out_shape=[jax.ShapeDtypeStruct((b, t, heads * dv), BF16 if rows == M_CHUNK else F32),
                   jax.ShapeDtypeStruct((b, heads, dk, dv), F32),
                   jax.ShapeDtypeStruct((b, heads, dk), F32),
                   jax.ShapeDtypeStruct((b, heads, LANES), F32)],
        scratch_shapes=[pltpu.VMEM((heads, dk, dv), F32), pltpu.VMEM((heads, dk), F32),
                        pltpu.VMEM((heads, LANES), F32)],
        compiler_params=_params(("arbitrary", "arbitrary")),
        name="mlstm_chunkwise",
    )(q, k, v_src, og_src, gates, bg, outnorm.reshape(1, dv), c0, n0, m0b)
    return o, c, n, m[:, :, 0]


def _outproj_kernel(a_ref, b_ref, wa_ref, wb_ref, x_ref, o_ref):
    o_ref[...] = x_ref[...] + jnp.dot(a_ref[...], wa_ref[...], preferred_element_type=F32) \
        + jnp.dot(b_ref[...], wb_ref[...], preferred_element_type=F32)


def _output_projection(a, bmix, w, x):
    n, wa = a.shape
    wb = bmix.shape[1]
    d = w.shape[1]
    tm = _row_tile(n, 512)
    tn = min(1024, d)
    return pl.pallas_call(
        _outproj_kernel,
        grid=(n // tm, d // tn),
        in_specs=[
            pl.BlockSpec((tm, wa), lambda i, j: (i, 0)),
            pl.BlockSpec((tm, wb), lambda i, j: (i, 0)),
            pl.BlockSpec((wa, tn), lambda i, j: (0, j)),
            pl.BlockSpec((wb, tn), lambda i, j: (wa // wb, j)),
            pl.BlockSpec((tm, tn), lambda i, j: (i, j)),
        ],
        out_specs=pl.BlockSpec((tm, tn), lambda i, j: (i, j)),
        out_shape=jax.ShapeDtypeStruct((n, d), F32),
        compiler_params=_params(("arbitrary", "arbitrary")),
        name="output_projection",
    )(a, bmix, w, w, x)


def _router_kernel(x_ref, g_ref, whi_ref, wlo_ref, b_ref, xf_ref, lg_ref):
    x = x_ref[...]
    ms = jnp.mean(x * x, axis=-1, keepdims=True)
    xn = (x * lax.rsqrt(ms + EPS)) * g_ref[...]
    hi = xn.astype(BF16)
    lo = (xn - hi.astype(F32)).astype(BF16)
    xf_ref[...] = hi
    lg_ref[...] = b_ref[...] + jnp.dot(hi, whi_ref[...], preferred_element_type=F32) \
        + (jnp.dot(hi, wlo_ref[...], preferred_element_type=F32)
           + jnp.dot(lo, whi_ref[...], preferred_element_type=F32))


def _ffn_norm_router(x, g, w_route, b_route):
    n, d = x.shape
    tm = _row_tile(n, 256)
    whi = w_route.astype(BF16)
    wlo = (w_route - whi.astype(F32)).astype(BF16)
    const = lambda shape: pl.BlockSpec(shape, lambda i: (0, 0))
    return pl.pallas_call(
        _router_kernel,
        grid=(n // tm,),
        in_specs=[pl.BlockSpec((tm, d), lambda i: (i, 0)), const((1, d)), const((d, LANES)),
                  const((d, LANES)), const((1, LANES))],
        out_specs=[pl.BlockSpec((tm, d), lambda i: (i, 0)), pl.BlockSpec((tm, LANES), lambda i: (i, 0))],
        out_shape=[jax.ShapeDtypeStruct((n, d), BF16), jax.ShapeDtypeStruct((n, LANES), F32)],
        compiler_params=_params(("arbitrary",)),
        name="ffn_norm_router",
    )(x, g.reshape(1, d), whi, wlo, b_route.reshape(1, LANES))


MOE_ROWS = 256
MOE_FF_TILE = 256


def _moe_kernel(be_ref, nu_ref, x_ref, wg_ref, wu_ref, wd_ref, rw_ref, o_ref):
    blk = pl.program_id(0)
    f = pl.program_id(1)

    @pl.when(blk < nu_ref[0])
    def _():
        x = x_ref[...]
        gt = jnp.dot(x, wg_ref[0].astype(BF16), preferred_element_type=F32)
        up = jnp.dot(x, wu_ref[0].astype(BF16), preferred_element_type=F32)
        hdn = (gt * jax.nn.sigmoid(gt)) * up
        y = jnp.dot(hdn.astype(BF16), wd_ref[0].astype(BF16), preferred_element_type=F32)

        @pl.when(f == 0)
        def _():
            o_ref[...] = y

        @pl.when(f > 0)
        def _():
            o_ref[...] += y

        @pl.when(f == pl.num_programs(1) - 1)
        def _():
            o_ref[...] = o_ref[...] * rw_ref[...]


def _moe_ffn(xb, row_w, block_e, n_used, w_gate, w_up, w_down):
    rows, d = xb.shape
    ff = w_gate.shape[2]
    tf = min(MOE_FF_TILE, ff)
    nf = ff // tf
    nblk = rows // MOE_ROWS

    def live(blk, nu):
        return jnp.minimum(blk, nu[0] - 1)

    def f_idx(blk, f, nu):
        return jnp.where(blk < nu[0], f, nf - 1)

    return pl.pallas_call(
        _moe_kernel,
        grid_spec=pltpu.PrefetchScalarGridSpec(
            num_scalar_prefetch=2,
            grid=(nblk, nf),
            in_specs=[
                pl.BlockSpec((MOE_ROWS, d), lambda blk, f, be, nu: (live(blk, nu), 0)),
                pl.BlockSpec((1, d, tf), lambda blk, f, be, nu: (be[live(blk, nu)], 0, f_idx(blk, f, nu))),
                pl.BlockSpec((1, d, tf), lambda blk, f, be, nu: (be[live(blk, nu)], 0, f_idx(blk, f, nu))),
                pl.BlockSpec((1, tf, d), lambda blk, f, be, nu: (be[live(blk, nu)], f_idx(blk, f, nu), 0)),
                pl.BlockSpec((MOE_ROWS, 1), lambda blk, f, be, nu: (live(blk, nu), 0)),
            ],
            out_specs=pl.BlockSpec((MOE_ROWS, d), lambda blk, f, be, nu: (live(blk, nu), 0)),
        ),
        out_shape=jax.ShapeDtypeStruct((rows, d), F32),
        compiler_params=_params(("arbitrary", "arbitrary")),
        name="moe_expert_ffn",
    )(block_e, n_used, xb, w_gate, w_up, w_down, row_w)


def _route(logits, n_groups, n_experts):
    per = n_experts // n_groups
    n = logits.shape[0]
    lg = logits[:, :n_groups]
    le = logits[:, n_groups:n_groups + n_experts]
    gval, gidx = lax.top_k(jax.nn.softmax(lg, axis=-1), 1)
    le = le.reshape(n, n_groups, per)[jnp.arange(n), gidx[:, 0]]
    ev, eidx = lax.top_k(le, TOP_K)
    wts = gval * jax.nn.softmax(ev, axis=-1)
    eid = (gidx * per + eidx).astype(jnp.int32)
    return eid, wts


def _dispatch(eid, n_experts, n_rows):
    flat = eid.reshape(-1)
    onehot = (flat[:, None] == jnp.arange(n_experts, dtype=jnp.int32)[None, :]).astype(jnp.int32)
    rank = jnp.take_along_axis(jnp.cumsum(onehot, axis=0) - onehot, flat[:, None], axis=1)[:, 0]
    counts = jnp.sum(onehot, axis=0)
    blocks = (counts + MOE_ROWS - 1) // MOE_ROWS
    bend = jnp.cumsum(blocks)
    bstart = bend - blocks
    slot = bstart[flat] * MOE_ROWS + rank
    n_used = bend[-1:].astype(jnp.int32)
    nblk = n_rows // MOE_ROWS
    block_e = jnp.searchsorted(bend, jnp.arange(nblk, dtype=jnp.int32), side='right')
    block_e = jnp.clip(block_e, 0, n_experts - 1).astype(jnp.int32)
    return slot.astype(jnp.int32), block_e, n_used


def _moe(x1, norm_ffn, w_group, b_group, w_router, b_router, w_gate, w_up, w_down):
    n, d = x1.shape
    n_groups = w_group.shape[1]
    n_experts = w_router.shape[1]
    used = n_groups + n_experts
    w_route = jnp.pad(jnp.concatenate([w_group, w_router], axis=1), ((0, 0), (0, LANES - used)))
    b_route = jnp.pad(jnp.concatenate([b_group, b_router]), (0, LANES - used))
    xf, logits = _ffn_norm_router(x1, norm_ffn, w_route, b_route)
    eid, wts = _route(logits, n_groups, n_experts)
    n_assign = n * TOP_K
    n_rows = (n_assign // MOE_ROWS + n_experts) * MOE_ROWS
    slot, block_e, n_used = _dispatch(eid, n_experts, n_rows)
    tok = jnp.repeat(jnp.arange(n, dtype=jnp.int32), TOP_K)
    row_tok = jnp.full((n_rows,), n, jnp.int32).at[slot].set(tok)
    row_w = jnp.zeros((n_rows,), F32).at[slot].set(wts.reshape(-1))
    xb = xf.at[row_tok].get(mode='fill', fill_value=0)
    yb = _moe_ffn(xb, row_w.reshape(n_rows, 1), block_e, n_used, w_gate, w_up, w_down)
    slot2 = slot.reshape(n, TOP_K)
    return x1 + (yb[slot2[:, 0]] + yb[slot2[:, 1]])


def _layer(xp, xs, conv_s, c_s, n_s, m_s, cache_k, cache_v, page_table, lambda_init, p):
    (norm_mix, w_in, b_gate, conv_w, conv_b, q_norm, k_norm, da_lambda, da_subln,
     m_outnorm, w_out, norm_ffn, w_group, b_group, w_router, b_router, w_gate, w_up, w_down) = p
    bp, tp, d = xp.shape
    bs, ts, _ = xs.shape
    n_p, n_s_rows = bp * tp, bs * ts
    heads_m, dk, dv = c_s.shape[1:]
    da_heads = cache_k.shape[2]
    da_w = da_heads * LANES
    m_w = heads_m * dv
    conv_c = conv_w.shape[1]
    assert da_w == m_w == conv_c, "the six projection regions are assumed equally wide"
    region = da_w
    n_reg = 6
    lp = da_lambda.astype(F32)
    lam = jnp.exp(jnp.sum(lp[0] * lp[1])) - jnp.exp(jnp.sum(lp[2] * lp[3])) + lambda_init
    out_scale = 1.0 - lambda_init

    x = jnp.concatenate([xp.reshape(n_p, d), xs.reshape(n_s_rows, d)], axis=0)
    w_main = w_in[:, :n_reg * region].astype(BF16)
    w_g = jnp.pad(w_in[:, n_reg * region:], ((0, 0), (0, LANES - 2 * heads_m))).astype(BF16)
    proj, gates = _input_projection(x, norm_mix, w_main, w_g, region)
    proj_s = proj[:, n_p:]

    q_bf, k_f32, k_bf, v_bf = _qk_prep(proj, q_norm, k_norm)
    o_da_p = _prompt_attention(q_bf[:n_p].reshape(bp, tp, da_w), k_bf[:n_p].reshape(bp, tp, da_w),
                               v_bf[:n_p].reshape(bp, tp, da_w), lam, da_subln, out_scale)
    n_pool, page = cache_k.shape[:2]
    o_da_s = _paged_attention(q_bf[n_p:].reshape(bs, ts, da_w), k_bf[n_p:].reshape(bs, ts, da_w),
                              v_bf[n_p:].reshape(bs, ts, da_w), cache_k.reshape(n_pool, page, da_w),
                              cache_v.reshape(n_pool, page, da_w), page_table, lam, da_subln, out_scale)

    q_scale = dk ** -0.5
    zeros_state = jnp.zeros((bp, SUBLANES, conv_c), F32)
    mq_p, mk_p = _conv_silu(proj, 3, bp, tp, zeros_state, conv_w, conv_b, q_scale)
    assert ts <= SUBLANES
    qk_s = proj_s[3].reshape(bs, ts, conv_c)
    xs16 = jnp.concatenate([jnp.zeros((bs, 2 * SUBLANES - ts - (CONV_W - 1), conv_c), F32), conv_s, qk_s], axis=1)
    xs16 = xs16.reshape(1, bs * 2 * SUBLANES, conv_c)
    mq_s, mk_s = _conv_silu(xs16, 0, bs, 2 * SUBLANES, jnp.zeros((bs, SUBLANES, conv_c), F32),
                            conv_w, conv_b, q_scale)
    pad8 = lambda a: jnp.pad(a, ((0, 0), (0, SUBLANES - ts), (0, 0)))
    mq_s = pad8(mq_s[:, 2 * SUBLANES - ts:]).astype(F32)
    mk_s = pad8(mk_s[:, 2 * SUBLANES - ts:]).astype(F32)

    def region_spec(r, nb_rows):
        return lambda rows, nc: pl.BlockSpec((1, rows, region), lambda bi, ci: (r, bi * nc + ci, 0))

    zc = jnp.zeros((bp, heads_m, dk, dv), F32)
    o_m_p, c_p, n_pp, m_p = _mlstm(mq_p, mk_p, proj, region_spec(4, n_p), proj, region_spec(5, n_p),
                                   gates[:n_p].reshape(bp, tp, LANES), b_gate, m_outnorm,
                                   zc, jnp.zeros((bp, heads_m, dk), F32), jnp.zeros((bp, heads_m), F32), M_CHUNK)
    seq_spec = lambda rows, nc: pl.BlockSpec((1, rows, region), lambda bi, ci: (bi, ci, 0))
    v_s8 = pad8(proj_s[4].reshape(bs, ts, region))
    og_s8 = pad8(proj_s[5].reshape(bs, ts, region))
    g_s8 = pad8(gates[n_p:].reshape(bs, ts, LANES))
    o_m_s, c_new, n_new, m_new = _mlstm(mq_s, mk_s, v_s8, seq_spec, og_s8, seq_spec, g_s8, b_gate, m_outnorm,
                                        c_s, n_s, m_s, ts)

    o_da = jnp.concatenate([o_da_p.reshape(n_p, da_w), o_da_s.reshape(n_s_rows, da_w)], axis=0)
    o_m = jnp.concatenate([o_m_p.reshape(n_p, m_w), o_m_s[:, :ts].reshape(n_s_rows, m_w).astype(BF16)], axis=0)
    x1 = _output_projection(o_da, o_m, w_out.astype(BF16), x)
    y = _moe(x1, norm_ffn, w_group, b_group, w_router, b_router, w_gate, w_up, w_down)

    k4 = lambda a, b_, t_: a.reshape(b_, t_, da_heads, LANES)
    conv_tail = lambda a, b_, t_: a.reshape(b_, t_, conv_c)[:, t_ - (CONV_W - 1):]
    conv_new_s = jnp.concatenate([conv_s, qk_s], axis=1)[:, ts:]
    outs_p = (k4(k_f32[:n_p], bp, tp), k4(proj[2, :n_p], bp, tp), c_p, n_pp, m_p, conv_tail(proj[3, :n_p], bp, tp))
    outs_s = (k4(k_f32[n_p:], bs, ts), k4(proj_s[2], bs, ts), c_new, n_new, m_new, conv_new_s)
    return y[:n_p].reshape(bp, tp, d), y[n_p:].reshape(bs, ts, d), outs_p, outs_s


def kernel(x_prompt, x_sample, cache_k, cache_v, state_C, state_n, state_m, state_conv, page_table, norm_mix, w_in, b_gate, conv_w, conv_b, q_norm, k_norm, da_lambda, da_subln, m_outnorm, w_out, norm_ffn, w_group, b_group, w_router, b_router, w_gate, w_up, w_down):
    depth = w_in.shape[0]
    yp, ys = x_prompt, x_sample
    outs = [[] for _ in range(12)]
    for l in range(depth):
        lambda_init = 0.8 - 0.6 * math.exp(-0.3 * l)
        p = (norm_mix[l], w_in[l], b_gate[l], conv_w[l], conv_b[l], q_norm[l], k_norm[l], da_lambda[l],
             da_subln[l], m_outnorm[l], w_out[l], norm_ffn[l], w_group[l], b_group[l], w_router[l],
             b_router[l], w_gate[l], w_up[l], w_down[l])
        yp, ys, outs_p, outs_s = _layer(yp, ys, state_conv[l], state_C[l], state_n[l], state_m[l],
                                        cache_k[l], cache_v[l], page_table, lambda_init, p)
        for i, a in enumerate(outs_p + outs_s):
            outs[i].append(a)
    return (yp, ys) + tuple(jnp.stack(o) for o in outs)
```

```python
import functools
import math

import numpy as np
import jax
import jax.numpy as jnp
from jax import lax
from jax.experimental import pallas as pl
from jax.experimental.pallas import tpu as pltpu

F32 = jnp.float32
BF16 = jnp.bfloat16

EPS = 1e-6
ALIBI_MAX = 8.0
CONV_W = 4
TOP_K = 2
M_CHUNK = 128
NEG = -1e30

LANES = 128
SUBLANES = 8
VMEM_BYTES_V7X = 64 * 1024 * 1024
VMEM_LIMIT = VMEM_BYTES_V7X - 8 * 1024 * 1024


def _row_tile(n, cap):
    t = cap
    while t > SUBLANES and n % t:
        t //= 2
    assert n % t == 0, (n, t)
    return t


def _params(sem, vmem=None):
    return pltpu.CompilerParams(dimension_semantics=sem, vmem_limit_bytes=vmem or VMEM_LIMIT)


def _proj_kernel(x_ref, g_ref, w_ref, wg_ref, o_ref, og_ref, xn_ref):
    @pl.when(pl.program_id(1) == 0)
    def _():
        x = x_ref[...]
        ms = jnp.mean(x * x, axis=-1, keepdims=True)
        xn = ((x * lax.rsqrt(ms + EPS)) * g_ref[...]).astype(BF16)
        xn_ref[...] = xn
        og_ref[...] = jnp.dot(xn, wg_ref[...], preferred_element_type=F32)

    o_ref[0] = jnp.dot(xn_ref[...], w_ref[...], preferred_element_type=F32)


def _input_projection(x, g, w, wg, region):
    n, d = x.shape
    e = w.shape[1]
    tm = _row_tile(n, 512)
    tn = min(1024, region)
    per = region // tn
    return pl.pallas_call(
        _proj_kernel,
        grid=(n // tm, e // tn),
        in_specs=[
            pl.BlockSpec((tm, d), lambda i, j: (i, 0)),
            pl.BlockSpec((1, d), lambda i, j: (0, 0)),
            pl.BlockSpec((d, tn), lambda i, j: (0, j)),
            pl.BlockSpec((d, LANES), lambda i, j: (0, 0)),
        ],
        out_specs=[
            pl.BlockSpec((1, tm, tn), lambda i, j: (j // per, i, j % per)),
            pl.BlockSpec((tm, LANES), lambda i, j: (i, 0)),
        ],
        out_shape=[
            jax.ShapeDtypeStruct((e // region, n, region), F32),
            jax.ShapeDtypeStruct((n, LANES), F32),
        ],
        scratch_shapes=[pltpu.VMEM((tm, d), BF16)],
        compiler_params=_params(("arbitrary", "arbitrary")),
        name="input_projection",
    )(x, g.reshape(1, d), w, wg)


SLOPE_TERMS = 4
FEAT = 2 * SLOPE_TERMS
POS_SPLIT = 64
LOG2E = math.log2(math.e)


def _alibi_slopes(n):
    return np.array([2.0 ** (-ALIBI_MAX * (h + 1) / n) for h in range(n)], np.float32)


def _slope_terms(heads):
    import ml_dtypes
    rest = _alibi_slopes(heads).astype(np.float64) * LOG2E
    terms = []
    for _ in range(SLOPE_TERMS):
        t = rest.astype(ml_dtypes.bfloat16).astype(np.float64)
        terms.append(t)
        rest = rest - t
    return np.stack(terms, axis=-1).astype(np.float32)


def _qkprep_kernel(q_ref, k_ref, v_ref, gq_ref, gk_ref, grp_ref, sf_ref, ko_ref, q2_ref, k2_ref, v2_ref,
                   *, heads, half, q_scale, tm, t_seq):
    grp = grp_ref[...]
    lane = lax.broadcasted_iota(jnp.int32, (1, LANES), 1)
    lo_half = lane < half
    fl = lane & (half - 1)
    pos = (pl.program_id(0) * tm) % t_seq + lax.broadcasted_iota(jnp.int32, (tm, 1), 0)
    pos_hi = (pos - (pos & (POS_SPLIT - 1))).astype(F32)
    pos_lo = (pos & (POS_SPLIT - 1)).astype(F32)
    kfeat = jnp.where(fl < FEAT, jnp.where((fl & 1) == 0, pos_hi, pos_lo), 0.0)
    ones_col = jnp.broadcast_to(jnp.where(lane == 0, 1.0, 0.0), (tm, LANES)).astype(BF16)

    def half_norm(x, g):
        sq = x * x
        hi = sq.astype(BF16)
        lo = (sq - hi.astype(F32)).astype(BF16)
        s = jnp.dot(hi, grp, preferred_element_type=F32) + jnp.dot(lo, grp, preferred_element_type=F32)
        return (x * lax.rsqrt(s * (1.0 / half) + EPS)) * g

    for h in range(heads):
        sl = slice(h * LANES, (h + 1) * LANES)
        m0 = slice(2 * h * LANES, (2 * h + 1) * LANES)
        m1 = slice((2 * h + 1) * LANES, (2 * h + 2) * LANES)
        qn = half_norm(q_ref[0, :, sl], gq_ref[...]) * q_scale
        sf = sf_ref[h:h + 1, :]
        q2_ref[:, m0] = jnp.where(lo_half, qn, sf).astype(BF16)
        q2_ref[:, m1] = jnp.where(lo_half, sf, qn).astype(BF16)
        kn = half_norm(k_ref[0, :, sl], gk_ref[...])
        ko_ref[:, sl] = kn
        k2_ref[:, m0] = jnp.where(lo_half, kn, kfeat).astype(BF16)
        k2_ref[:, m1] = jnp.where(lo_half, kfeat, kn).astype(BF16)
        v2_ref[:, m0] = v_ref[0, :, sl].astype(BF16)
        v2_ref[:, m1] = ones_col


def _qk_prep(proj, q_norm, k_norm, t_seq):
    _, n, w = proj.shape
    heads = w // LANES
    half = q_norm.shape[0]
    assert 2 * half == LANES and FEAT <= half
    tm = _row_tile(math.gcd(n, t_seq), 256)
    assert t_seq <= POS_SPLIT * 256, "key positions must split into two bf16-exact factors"
    lane = np.arange(LANES)
    grp = jnp.asarray((lane[:, None] // half == lane[None, :] // half).astype(np.float32), BF16)
    gq = jnp.concatenate([q_norm, q_norm]).reshape(1, LANES)
    gk = jnp.concatenate([k_norm, k_norm]).reshape(1, LANES)
    terms = _slope_terms(heads)
    fl = lane % half
    sf = jnp.asarray(np.where(fl[None, :] < FEAT, terms[:, np.minimum(fl // 2, SLOPE_TERMS - 1)], 0.0))
    row = lambda r: pl.BlockSpec((1, tm, w), lambda i: (r, i, 0))
    full = lambda shape: pl.BlockSpec(shape, lambda i: (0, 0))
    out1 = pl.BlockSpec((tm, w), lambda i: (i, 0))
    out2 = pl.BlockSpec((tm, 2 * w), lambda i: (i, 0))
    return pl.pallas_call(
        functools.partial(_qkprep_kernel, heads=heads, half=half, q_scale=half ** -0.5 * LOG2E,
                          tm=tm, t_seq=t_seq),
        grid=(n // tm,),
        in_specs=[row(0), row(1), row(2), full((1, LANES)), full((1, LANES)), full((LANES, LANES)),
                  full((heads, LANES))],
        out_specs=[out1, out2, out2, out2],
        out_shape=[
            jax.ShapeDtypeStruct((n, w), F32),
            jax.ShapeDtypeStruct((n, 2 * w), BF16),
            jax.ShapeDtypeStruct((n, 2 * w), BF16),
            jax.ShapeDtypeStruct((n, 2 * w), BF16),
        ],
        compiler_params=_params(("arbitrary",)),
        name="qk_prep",
    )(proj, proj, proj, gq, gk, grp, sf)


def _subln(o, g, out_scale):
    ms = jnp.mean(o * o, axis=-1, keepdims=True)
    return ((o * lax.rsqrt(ms + EPS)) * g) * out_scale


def _attn_kernel(lam_ref, q_ref, k_ref, v_ref, gs_ref, o_ref, m0_ref, a0_ref, m1_ref, a1_ref,
                 *, tq, out_scale):
    i = pl.program_id(2)
    q = q_ref[...]
    qs = (q[:, :LANES], q[:, LANES:])
    states = ((m0_ref, a0_ref), (m1_ref, a1_ref))
    for m_ref, a_ref in states:
        m_ref[...] = jnp.full(m_ref.shape, NEG, F32)
        a_ref[...] = jnp.zeros(a_ref.shape, F32)
    causal = lax.broadcasted_iota(jnp.int32, (tq, tq), 0) >= lax.broadcasted_iota(jnp.int32, (tq, tq), 1)

    def tile(j, masked):
        start = pl.multiple_of(j * tq, tq)
        kt = k_ref[pl.ds(start, tq), :]
        vt = v_ref[pl.ds(start, tq), :]
        for c, (m_ref, a_ref) in enumerate(states):
            s = lax.dot_general(qs[c], kt[:, c * LANES:(c + 1) * LANES], (((1,), (1,)), ((), ())),
                                preferred_element_type=F32)
            if masked:
                s = jnp.where(causal, s, NEG)
            m_old = m_ref[...]
            m_new = jnp.maximum(m_old, jnp.max(s, axis=-1, keepdims=True))
            p = jnp.exp2(s - m_new)
            a_ref[...] = jnp.exp2(m_old - m_new) * a_ref[...] + jnp.dot(p.astype(BF16), vt,
                                                                         preferred_element_type=F32)
            m_ref[...] = m_new

    def body(j, carry):
        tile(j, False)
        return carry

    lax.fori_loop(0, i, body, 0)
    tile(i, True)
    a0 = a0_ref[...]
    a1 = a1_ref[...]
    o = a0[:, :LANES] / a0[:, LANES:LANES + 1] - lam_ref[0] * (a1[:, :LANES] / a1[:, LANES:LANES + 1])
    o_ref[...] = _subln(o, gs_ref[...], out_scale).astype(o_ref.dtype)


def _prompt_attention(q2, k2, v2, n_batch, t, lam, subln, out_scale):
    heads = q2.shape[1] // (2 * LANES)
    tq = _row_tile(t, 512)
    nq = t // tq
    kv = pl.BlockSpec((t, 2 * LANES), lambda bi, hi, i, lm: (bi, hi))
    return pl.pallas_call(
        functools.partial(_attn_kernel, tq=tq, out_scale=out_scale),
        grid_spec=pltpu.PrefetchScalarGridSpec(
            num_scalar_prefetch=1,
            grid=(n_batch, heads, nq),
            in_specs=[pl.BlockSpec((tq, 2 * LANES), lambda bi, hi, i, lm: (bi * nq + i, hi)), kv, kv,
                      pl.BlockSpec((1, LANES), lambda bi, hi, i, lm: (0, 0))],
            out_specs=pl.BlockSpec((tq, LANES), lambda bi, hi, i, lm: (bi * nq + i, hi)),
            scratch_shapes=[pltpu.VMEM((tq, 1), F32), pltpu.VMEM((tq, 2 * LANES), F32)] * 2,
        ),
        out_shape=jax.ShapeDtypeStruct((n_batch * t, heads * LANES), BF16),
        compiler_params=_params(("arbitrary", "arbitrary", "arbitrary")),
        name="prompt_diff_attention",
    )(lam.reshape(1), q2, k2, v2, subln.reshape(1, LANES))


QSLOTS = 4
NEW_SLOTS = 8


def _paged_attn_kernel(pt_ref, lam_ref, *refs, pages, page, out_scale):
    kp_refs = refs[:pages]
    vp_refs = refs[pages:2 * pages]
    (q_ref, kn_ref, vn_ref, bias_ref, biasn_ref, slope_ref, gs_ref, o_ref, m_ref, l_ref, a_ref) = refs[2 * pages:]
    g = pl.program_id(1)
    q = q_ref[0]
    slope_c = slope_ref[...]
    nt = (((1,), (1,)), ((), ()))

    def update(s, shift_c, v_tile, first):
        m_old = jnp.full(m_ref.shape, NEG, F32) if first else m_ref[...]
        m_new = jnp.maximum(m_old, jnp.max(s, axis=-1, keepdims=True) + shift_c)
        p = jnp.exp2(s - (m_new - shift_c))
        psum = jnp.sum(p, axis=-1, keepdims=True)
        pv = jnp.dot(p.astype(BF16), v_tile, preferred_element_type=F32)
        if first:
            l_ref[...] = psum
            a_ref[...] = pv
        else:
            alpha = jnp.exp2(m_old - m_new)
            l_ref[...] = alpha * l_ref[...] + psum
            a_ref[...] = alpha * a_ref[...] + pv
        m_ref[...] = m_new

    @pl.when(g == 0)
    def _():
        s = lax.dot_general(q, kn_ref[0], nt, preferred_element_type=F32) + biasn_ref[...]
        update(s, jnp.zeros_like(slope_c), vn_ref[0], True)

    for p in range(pages):
        kt = kp_refs[p][0].astype(BF16)
        vt = vp_refs[p][0].astype(BF16)
        s = lax.dot_general(q, kt, nt, preferred_element_type=F32) + bias_ref[...]
        first_pos = ((g * pages + p) * page).astype(F32)
        update(s, slope_c * first_pos, vt, False)

    @pl.when(g == pl.num_programs(1) - 1)
    def _():
        a = a_ref[...] / l_ref[...]
        half_rows = a.shape[0] // 2
        o = a[:half_rows] - lam_ref[0] * a[half_rows:]
        o_ref[0] = _subln(o, gs_ref[...], out_scale)


def _paged_attention(q2, k2, v2, cache_k, cache_v, page_table, lam, subln, out_scale):
    b, tq, _ = q2.shape
    n_pool, page, heads, _ = cache_k.shape
    half = LANES // 2
    n_pages = page_table.shape[1]
    past_len = n_pages * page
    pages = math.gcd(n_pages, 4)
    rows = 2 * heads * QSLOTS
    assert tq <= QSLOTS and tq <= NEW_SLOTS
    q5 = q2.reshape(b, tq, heads, 2, LANES)
    zero = jnp.zeros((b, tq, heads, half), BF16)
    qmaps = jnp.stack([jnp.concatenate([q5[:, :, :, 0, :half], zero], axis=-1),
                       jnp.concatenate([zero, q5[:, :, :, 1, half:]], axis=-1)], axis=1)
    qrows = jnp.pad(qmaps.transpose(0, 1, 3, 2, 4), ((0, 0), (0, 0), (0, 0), (0, QSLOTS - tq), (0, 0)))
    qrows = qrows.reshape(b, rows, LANES)
    k5 = k2.reshape(b, tq, heads, 2, LANES)
    kn = jnp.concatenate([k5[:, :, :, 0, :half], k5[:, :, :, 1, half:]], axis=-1)
    vn = v2.reshape(b, tq, heads, 2, LANES)[:, :, :, 0, :]
    pad_t = ((0, 0), (0, NEW_SLOTS - tq), (0, 0), (0, 0))
    kn = jnp.pad(kn, pad_t).reshape(b, NEW_SLOTS * heads, LANES)
    vn = jnp.pad(vn, pad_t).reshape(b, NEW_SLOTS * heads, LANES)
    slope2 = _slope_terms(heads).astype(np.float64).sum(axis=-1)
    r = np.arange(rows)
    row_head = (r // QSLOTS) % heads
    row_q = r % QSLOTS

    def bias_matrix(n_tok, key_pos0):
        col = np.arange(n_tok * heads)
        col_tok, col_head = col // heads, col % heads
        dist = (past_len + row_q)[:, None] - (key_pos0 + col_tok)[None, :]
        ok = (row_head[:, None] == col_head[None, :]) & (dist >= 0)
        return jnp.asarray(np.where(ok, -slope2[row_head][:, None] * dist, NEG).astype(np.float32))

    bias_page = bias_matrix(page, 0)
    new_tok = np.arange(NEW_SLOTS * heads) // heads
    bias_new = jnp.where(jnp.asarray(new_tok < tq)[None, :], bias_matrix(NEW_SLOTS, past_len), NEG)
    slope_col = jnp.asarray(slope2[row_head].astype(np.float32).reshape(rows, 1))
    ck = cache_k.reshape(n_pool, page * heads, LANES)
    cv = cache_v.reshape(n_pool, page * heads, LANES)

    def page_spec(p):
        return pl.BlockSpec((1, page * heads, LANES),
                            lambda bi, g, pt, lm: (pt[bi * n_pages + g * pages + p], 0, 0))

    per_b = lambda shape: pl.BlockSpec(shape, lambda bi, g, pt, lm: (bi,) + (0,) * (len(shape) - 1))
    const = lambda shape: pl.BlockSpec(shape, lambda bi, g, pt, lm: (0,) * len(shape))
    out = pl.pallas_call(
        functools.partial(_paged_attn_kernel, pages=pages, page=page, out_scale=out_scale),
        grid_spec=pltpu.PrefetchScalarGridSpec(
            num_scalar_prefetch=2,
            grid=(b, n_pages // pages),
            in_specs=[page_spec(p) for p in range(pages)] * 2 + [
                per_b((1, rows, LANES)), per_b((1, NEW_SLOTS * heads, LANES)), per_b((1, NEW_SLOTS * heads, LANES)),
                const((rows, page * heads)), const((rows, NEW_SLOTS * heads)), const((rows, 1)),
                const((1, LANES))],
            out_specs=per_b((1, rows // 2, LANES)),
            scratch_shapes=[pltpu.VMEM((rows, 1), F32), pltpu.VMEM((rows, 1), F32),
                            pltpu.VMEM((rows, LANES), F32)],
        ),
        out_shape=jax.ShapeDtypeStruct((b, rows // 2, LANES), F32),
        compiler_params=_params(("arbitrary", "arbitrary")),
        name="paged_diff_attention",
    )(page_table.reshape(-1), lam.reshape(1), *([ck] * pages), *([cv] * pages),
      qrows, kn, vn, bias_page, bias_new, slope_col, subln.reshape(1, LANES))
    return out.reshape(b, heads, QSLOTS, LANES)[:, :, :tq].transpose(0, 2, 1, 3).reshape(b, tq, heads * LANES)


def _conv_kernel(x_ref, st_ref, w_ref, b_ref, qo_ref, ko_ref, buf_ref, *, tt, q_scale, carry):
    @pl.when(pl.program_id(1) == 0)
    def _():
        buf_ref[0:SUBLANES, :] = st_ref[0]

    x = x_ref[0]
    buf_ref[SUBLANES:SUBLANES + tt, :] = x
    acc = b_ref[...] + w_ref[CONV_W - 1:CONV_W, :] * x
    for k in range(1, CONV_W):
        acc = acc + w_ref[CONV_W - 1 - k:CONV_W - k, :] * buf_ref[SUBLANES - k:SUBLANES - k + tt, :]
    u = acc * jax.nn.sigmoid(acc)
    c = u.shape[1] // 2
    qo_ref[0] = (u[:, :c] * q_scale).astype(qo_ref.dtype)
    ko_ref[0] = u[:, c:].astype(ko_ref.dtype)
    if carry:
        buf_ref[0:SUBLANES, :] = buf_ref[tt:tt + SUBLANES, :]


def _conv_silu(x, region, n_batch, t, state8, conv_w, conv_b, q_scale):
    c = x.shape[2]
    tt = _row_tile(t, 256)
    nt = t // tt
    w8 = jnp.pad(conv_w, ((0, SUBLANES - CONV_W), (0, 0)))
    out = pl.BlockSpec((1, tt, c // 2), lambda bi, i: (bi, i, 0))
    return pl.pallas_call(
        functools.partial(_conv_kernel, tt=tt, q_scale=q_scale, carry=nt > 1),
        grid=(n_batch, nt),
        in_specs=[
            pl.BlockSpec((1, tt, c), lambda bi, i: (region, bi * nt + i, 0)),
            pl.BlockSpec((1, SUBLANES, c), lambda bi, i: (bi, 0, 0)),
            pl.BlockSpec((SUBLANES, c), lambda bi, i: (0, 0)),
            pl.BlockSpec((1, c), lambda bi, i: (0, 0)),
        ],
        out_specs=[out, out],
        out_shape=[jax.ShapeDtypeStruct((n_batch, t, c // 2), BF16)] * 2,
        scratch_shapes=[pltpu.VMEM((tt + SUBLANES, c), F32)],
        compiler_params=_params(("arbitrary", "arbitrary")),
        name="mlstm_conv_silu",
    )(x, state8, w8, conv_b.reshape(1, c))


def _mlstm_kernel(q_ref, k_ref, v_ref, og_ref, g_ref, bg_ref, on_ref, c0_ref, n0_ref, m0_ref,
                  o_ref, c_ref, n_ref, m_ref, cs_ref, ns_ref, ms_ref, *, heads, dk, dv, rows, t_valid):
    L = M_CHUNK
    ci = pl.program_id(1)

    @pl.when(ci == 0)
    def _():
        cs_ref[...] = c0_ref[0]
        ns_ref[...] = n0_ref[0]
        ms_ref[...] = m0_ref[0]

    def pad_rows(a):
        if rows == L:
            return a
        return jnp.concatenate([a, jnp.zeros((L - rows, a.shape[1]), a.dtype)], axis=0)

    gates = pad_rows(g_ref[0] + bg_ref[...])
    lf_all = jnp.minimum(gates, 0.0) - jnp.log1p(jnp.exp(-jnp.abs(gates)))
    q_all = pad_rows(q_ref[0]).astype(BF16)
    k_all = pad_rows(k_ref[0]).astype(BF16)
    v_all = pad_rows(v_ref[0]).astype(BF16)
    r_i = lax.broadcasted_iota(jnp.int32, (L, L), 0)
    c_i = lax.broadcasted_iota(jnp.int32, (L, L), 1)
    eye = r_i == c_i
    tril = c_i <= r_i
    valid = lax.broadcasted_iota(jnp.int32, (L, 1), 0) < t_valid

    def to_row(col):
        return jnp.sum(jnp.where(eye, col, 0.0), axis=0, keepdims=True)

    for h in range(heads):
        ig_c = jnp.where(valid, gates[:, h:h + 1], NEG)
        lf_c = jnp.where(valid, lf_all[:, heads + h:heads + h + 1], 0.0)
        ig_r = to_row(ig_c)
        lf_r = to_row(lf_c)
        b_c = jnp.sum(jnp.where(tril, lf_r, 0.0), axis=1, keepdims=True)
        b_r = to_row(b_c)
        m_prev = ms_ref[h:h + 1, 0:1]
        dmat = jnp.where(tril, (b_c - b_r) + ig_r, NEG)
        inter = b_c + m_prev
        mt = jnp.maximum(inter, jnp.max(dmat, axis=1, keepdims=True))
        qh = q_all[:, h * dk:(h + 1) * dk]
        kh = k_all[:, h * dk:(h + 1) * dk]
        vh = v_all[:, h * dv:(h + 1) * dv]
        c_old = cs_ref[h]
        n_old = ns_ref[h:h + 1, :]
        s = lax.dot_general(qh, kh, (((1,), (1,)), ((), ())), preferred_element_type=F32)
        wm = s * jnp.exp(dmat - mt)
        e_in = jnp.exp(inter - mt)
        num = e_in * jnp.dot(qh, c_old.astype(BF16), preferred_element_type=F32) \
            + jnp.dot(wm.astype(BF16), vh, preferred_element_type=F32)
        den = e_in * jnp.sum(qh.astype(F32) * n_old, axis=1, keepdims=True) + jnp.sum(wm, axis=1, keepdims=True)
        hid = num / jnp.maximum(jnp.abs(den), jnp.exp(-mt))
        ms_h = jnp.mean(hid * hid, axis=-1, keepdims=True)
        hn = (hid * lax.rsqrt(ms_h + EPS)) * on_ref[...]
        gate = jax.nn.sigmoid(og_ref[0, :, h * dv:(h + 1) * dv])
        o_ref[0, :, h * dv:(h + 1) * dv] = (gate * hn[:rows]).astype(o_ref.dtype)
        b_last = jnp.sum(lf_r, axis=1, keepdims=True)
        g_c = (b_last - b_c) + ig_c
        g_r = (b_last - b_r) + ig_r
        m_new = jnp.maximum(b_last + m_prev, jnp.max(g_r, axis=1, keepdims=True))
        e_old = jnp.exp((b_last + m_prev) - m_new)
        kw = kh.astype(F32) * jnp.exp(g_c - m_new)
        cs_ref[h] = e_old * c_old + lax.dot_general(kw.astype(BF16), vh, (((0,), (0,)), ((), ())),
                                                    preferred_element_type=F32)
        ns_ref[h:h + 1, :] = e_old * n_old + jnp.sum(kw, axis=0, keepdims=True)
        ms_ref[h:h + 1, :] = jnp.broadcast_to(m_new, (1, LANES))

    @pl.when(ci == pl.num_programs(1) - 1)
    def _():
        c_ref[0] = cs_ref[...]
        n_ref[0] = ns_ref[...]
        m_ref[0] = ms_ref[...]


def _mlstm(q, k, v_src, v_spec, og_src, og_spec, gates, b_gate, outnorm, c0, n0, m0, t_valid):
    b, t, _ = q.shape
    heads, dk, dv = c0.shape[1:]
    rows = min(t, M_CHUNK)
    nc = t // rows
    bg = jnp.pad(b_gate, (0, LANES - b_gate.shape[0])).reshape(1, LANES)
    m0b = jnp.broadcast_to(m0[:, :, None], (b, heads, LANES))
    blk = lambda w: pl.BlockSpec((1, rows, w), lambda bi, ci: (bi, ci, 0))
    per_b = lambda shape: pl.BlockSpec(shape, lambda bi, ci: (bi,) + (0,) * (len(shape) - 1))
    const = lambda shape: pl.BlockSpec(shape, lambda bi, ci: (0,) * len(shape))
    kernel = functools.partial(_mlstm_kernel, heads=heads, dk=dk, dv=dv, rows=rows, t_valid=t_valid)
    o, c, n, m = pl.pallas_call(
        kernel,
        grid=(b, nc),
        in_specs=[blk(heads * dk), blk(heads * dk), v_spec(rows, nc), og_spec(rows, nc), blk(LANES),
                  const((1, LANES)), const((1, dv)), per_b((1, heads, dk, dv)), per_b((1, heads, dk)),
                  per_b((1, heads, LANES))],
        out_specs=[blk(heads * dv), per_b((1, heads, dk, dv)), per_b((1, heads, dk)), per_b((1, heads, LANES))],
        out_shape=[jax.ShapeDtypeStruct((b, t, heads * dv), BF16 if rows == M_CHUNK else F32),
                   jax.ShapeDtypeStruct((b, heads, dk, dv), F32),
                   jax.ShapeDtypeStruct((b, heads, dk), F32),
                   jax.ShapeDtypeStruct((b, heads, LANES), F32)],
        scratch_shapes=[pltpu.VMEM((heads, dk, dv), F32), pltpu.VMEM((heads, dk), F32),
                        pltpu.VMEM((heads, LANES), F32)],
        compiler_params=_params(("arbitrary", "arbitrary")),
        name="mlstm_chunkwise",
    )(q, k, v_src, og_src, gates, bg, outnorm.reshape(1, dv), c0, n0, m0b)
    return o, c, n, m[:, :, 0]


def _outproj_kernel(a_ref, b_ref, wa_ref, wb_ref, x_ref, o_ref):
    o_ref[...] = x_ref[...] + jnp.dot(a_ref[...], wa_ref[...], preferred_element_type=F32) \
        + jnp.dot(b_ref[...], wb_ref[...], preferred_element_type=F32)


def _output_projection(a, bmix, w, x):
    n, wa = a.shape
    wb = bmix.shape[1]
    d = w.shape[1]
    tm = _row_tile(n, 512)
    tn = min(1024, d)
    return pl.pallas_call(
        _outproj_kernel,
        grid=(n // tm, d // tn),
        in_specs=[
            pl.BlockSpec((tm, wa), lambda i, j: (i, 0)),
            pl.BlockSpec((tm, wb), lambda i, j: (i, 0)),
            pl.BlockSpec((wa, tn), lambda i, j: (0, j)),
            pl.BlockSpec((wb, tn), lambda i, j: (wa // wb, j)),
            pl.BlockSpec((tm, tn), lambda i, j: (i, j)),
        ],
        out_specs=pl.BlockSpec((tm, tn), lambda i, j: (i, j)),
        out_shape=jax.ShapeDtypeStruct((n, d), F32),
        compiler_params=_params(("arbitrary", "arbitrary")),
        name="output_projection",
    )(a, bmix, w, w, x)


def _router_kernel(x_ref, g_ref, whi_ref, wlo_ref, b_ref, xf_ref, lg_ref):
    x = x_ref[...]
    ms = jnp.mean(x * x, axis=-1, keepdims=True)
    xn = (x * lax.rsqrt(ms + EPS)) * g_ref[...]
    hi = xn.astype(BF16)
    lo = (xn - hi.astype(F32)).astype(BF16)
    xf_ref[...] = xn
    lg_ref[...] = b_ref[...] + jnp.dot(hi, whi_ref[...], preferred_element_type=F32) \
        + (jnp.dot(hi, wlo_ref[...], preferred_element_type=F32)
           + jnp.dot(lo, whi_ref[...], preferred_element_type=F32))


def _ffn_norm_router(x, g, w_route, b_route):
    n, d = x.shape
    tm = _row_tile(n, 256)
    whi = w_route.astype(BF16)
    wlo = (w_route - whi.astype(F32)).astype(BF16)
    const = lambda shape: pl.BlockSpec(shape, lambda i: (0, 0))
    return pl.pallas_call(
        _router_kernel,
        grid=(n // tm,),
        in_specs=[pl.BlockSpec((tm, d), lambda i: (i, 0)), const((1, d)), const((d, LANES)),
                  const((d, LANES)), const((1, LANES))],
        out_specs=[pl.BlockSpec((tm, d), lambda i: (i, 0)), pl.BlockSpec((tm, LANES), lambda i: (i, 0))],
        out_shape=[jax.ShapeDtypeStruct((n, d), F32), jax.ShapeDtypeStruct((n, LANES), F32)],
        compiler_params=_params(("arbitrary",)),
        name="ffn_norm_router",
    )(x, g.reshape(1, d), whi, wlo, b_route.reshape(1, LANES))


MOE_ROWS = 256
MOE_FF_TILE = 256


def _row_copy(src_ref, src_row, dst_ref, dst_row, sem):
    return pltpu.make_async_copy(src_ref.at[pl.ds(src_row, 1)], dst_ref.at[pl.ds(dst_row, 1)], sem)


def _gather_rows_kernel(ids_ref, src_ref, dst_ref, sem, *, rows):
    base = pl.program_id(0) * rows

    def start(r, carry):
        _row_copy(src_ref, ids_ref[0, 0, r], dst_ref, base + r, sem).start()
        return carry

    def wait(r, carry):
        _row_copy(src_ref, 0, dst_ref, base + r, sem).wait()
        return carry

    lax.fori_loop(0, rows, start, 0)
    lax.fori_loop(0, rows, wait, 0)


def _gather_rows(src, ids):
    n_rows = ids.shape[0]
    nblk = n_rows // MOE_ROWS
    return pl.pallas_call(
        functools.partial(_gather_rows_kernel, rows=MOE_ROWS),
        grid=(nblk,),
        in_specs=[pl.BlockSpec((1, 1, MOE_ROWS), lambda i: (i, 0, 0), memory_space=pltpu.SMEM),
                  pl.BlockSpec(memory_space=pl.ANY)],
        out_specs=pl.BlockSpec(memory_space=pl.ANY),
        out_shape=jax.ShapeDtypeStruct((n_rows, src.shape[1]), src.dtype),
        scratch_shapes=[pltpu.SemaphoreType.DMA(())],
        compiler_params=_params(("arbitrary",)),
        name="moe_gather_rows",
    )(ids.reshape(nblk, 1, MOE_ROWS), src)


COMBINE_ROWS = 128


def _combine_kernel(slots_ref, x_ref, yb_ref, o_ref, buf_ref, sem, *, tc):
    n_copies = TOP_K * tc

    def start(r, carry):
        _row_copy(yb_ref, slots_ref[0, 0, r], buf_ref, r, sem).start()
        return carry

    def wait(r, carry):
        _row_copy(yb_ref, 0, buf_ref, r, sem).wait()
        return carry

    lax.fori_loop(0, n_copies, start, 0)
    lax.fori_loop(0, n_copies, wait, 0)
    acc = buf_ref[0:tc, :]
    for k in range(1, TOP_K):
        acc = acc + buf_ref[k * tc:(k + 1) * tc, :]
    o_ref[...] = x_ref[...] + acc


def _combine(x, yb, slot2):
    n, d = x.shape
    tc = _row_tile(n, COMBINE_ROWS)
    nblk = n // tc
    slots = slot2.reshape(nblk, tc, TOP_K).transpose(0, 2, 1).reshape(nblk, 1, TOP_K * tc)
    return pl.pallas_call(
        functools.partial(_combine_kernel, tc=tc),
        grid=(nblk,),
        in_specs=[pl.BlockSpec((1, 1, TOP_K * tc), lambda i: (i, 0, 0), memory_space=pltpu.SMEM),
                  pl.BlockSpec((tc, d), lambda i: (i, 0)),
                  pl.BlockSpec(memory_space=pl.ANY)],
        out_specs=pl.BlockSpec((tc, d), lambda i: (i, 0)),
        out_shape=jax.ShapeDtypeStruct((n, d), F32),
        scratch_shapes=[pltpu.VMEM((TOP_K * tc, d), F32), pltpu.SemaphoreType.DMA(())],
        compiler_params=_params(("arbitrary",)),
        name="moe_combine",
    )(slots, x, yb)


def _moe_kernel(be_ref, nu_ref, x_ref, wg_ref, wu_ref, wd_ref, rw_ref, o_ref, xb_ref):
    blk = pl.program_id(0)
    f = pl.program_id(1)

    @pl.when(blk >= nu_ref[0])
    def _():
        o_ref[...] = jnp.zeros(o_ref.shape, F32)

    @pl.when(blk < nu_ref[0])
    def _():
        @pl.when(f == 0)
        def _():
            xb_ref[...] = x_ref[...].astype(BF16)

        x = xb_ref[...]
        gt = jnp.dot(x, wg_ref[0].astype(BF16), preferred_element_type=F32)
        up = jnp.dot(x, wu_ref[0].astype(BF16), preferred_element_type=F32)
        hdn = (gt * jax.nn.sigmoid(gt)) * up
        y = jnp.dot(hdn.astype(BF16), wd_ref[0].astype(BF16), preferred_element_type=F32)

        @pl.when(f == 0)
        def _():
            o_ref[...] = y

        @pl.when(f > 0)
        def _():
            o_ref[...] += y

        @pl.when(f == pl.num_programs(1) - 1)
        def _():
            o_ref[...] = o_ref[...] * rw_ref[...]


def _moe_ffn(xb, row_w, block_e, n_used, w_gate, w_up, w_down):
    rows, d = xb.shape
    ff = w_gate.shape[2]
    tf = min(MOE_FF_TILE, ff)
    nf = ff // tf
    nblk = rows // MOE_ROWS

    def live(blk, nu):
        return jnp.minimum(blk, nu[0] - 1)

    def f_idx(blk, f, nu):
        return jnp.where(blk < nu[0], f, nf - 1)

    return pl.pallas_call(
        _moe_kernel,
        grid_spec=pltpu.PrefetchScalarGridSpec(
            num_scalar_prefetch=2,
            grid=(nblk, nf),
            in_specs=[
                pl.BlockSpec((MOE_ROWS, d), lambda blk, f, be, nu: (live(blk, nu), 0)),
                pl.BlockSpec((1, d, tf), lambda blk, f, be, nu: (be[live(blk, nu)], 0, f_idx(blk, f, nu))),
                pl.BlockSpec((1, d, tf), lambda blk, f, be, nu: (be[live(blk, nu)], 0, f_idx(blk, f, nu))),
                pl.BlockSpec((1, tf, d), lambda blk, f, be, nu: (be[live(blk, nu)], f_idx(blk, f, nu), 0)),
                pl.BlockSpec((MOE_ROWS, 1), lambda blk, f, be, nu: (live(blk, nu), 0)),
            ],
            out_specs=pl.BlockSpec((MOE_ROWS, d), lambda blk, f, be, nu: (blk, 0)),
            scratch_shapes=[pltpu.VMEM((MOE_ROWS, d), BF16)],
        ),
        out_shape=jax.ShapeDtypeStruct((rows, d), F32),
        compiler_params=_params(("arbitrary", "arbitrary")),
        name="moe_expert_ffn",
    )(block_e, n_used, xb, w_gate, w_up, w_down, row_w)


def _route(logits, n_groups, n_experts):
    per = n_experts // n_groups
    n = logits.shape[0]
    lg = logits[:, :n_groups]
    le = logits[:, n_groups:n_groups + n_experts]
    gsm = jax.nn.softmax(lg, axis=-1)
    gidx = jnp.argmax(gsm, axis=-1)
    gval = jnp.max(gsm, axis=-1)
    le = jnp.take_along_axis(le.reshape(n, n_groups, per), gidx[:, None, None], axis=1)[:, 0]
    assert TOP_K == 2
    i1 = jnp.argmax(le, axis=-1)
    rest = jnp.where(jnp.arange(per)[None, :] == i1[:, None], -jnp.inf, le)
    i2 = jnp.argmax(rest, axis=-1)
    ev = jnp.stack([jnp.max(le, axis=-1), jnp.max(rest, axis=-1)], axis=-1)
    wts = gval[:, None] * jax.nn.softmax(ev, axis=-1)
    eid = (gidx[:, None] * per + jnp.stack([i1, i2], axis=-1)).astype(jnp.int32)
    return eid, wts


def _dispatch(eid, n_experts, n_rows):
    flat = eid.reshape(-1)
    onehot = (flat[:, None] == jnp.arange(n_experts, dtype=jnp.int32)[None, :]).astype(jnp.int32)
    rank = jnp.take_along_axis(jnp.cumsum(onehot, axis=0) - onehot, flat[:, None], axis=1)[:, 0]
    counts = jnp.sum(onehot, axis=0)
    blocks = (counts + MOE_ROWS - 1) // MOE_ROWS
    bend = jnp.cumsum(blocks)
    bstart = bend - blocks
    slot = bstart[flat] * MOE_ROWS + rank
    n_used = bend[-1:].astype(jnp.int32)
    nblk = n_rows // MOE_ROWS
    block_e = jnp.searchsorted(bend, jnp.arange(nblk, dtype=jnp.int32), side='right')
    block_e = jnp.clip(block_e, 0, n_experts - 1).astype(jnp.int32)
    return slot.astype(jnp.int32), block_e, n_used


def _moe(x1, norm_ffn, w_group, b_group, w_router, b_router, w_gate, w_up, w_down):
    n, d = x1.shape
    n_groups = w_group.shape[1]
    n_experts = w_router.shape[1]
    used = n_groups + n_experts
    w_route = jnp.pad(jnp.concatenate([w_group, w_router], axis=1), ((0, 0), (0, LANES - used)))
    b_route = jnp.pad(jnp.concatenate([b_group, b_router]), (0, LANES - used))
    xf, logits = _ffn_norm_router(x1, norm_ffn, w_route, b_route)
    eid, wts = _route(logits, n_groups, n_experts)
    n_assign = n * TOP_K
    n_rows = (n_assign // MOE_ROWS + n_experts) * MOE_ROWS
    slot, block_e, n_used = _dispatch(eid, n_experts, n_rows)
    tok = jnp.repeat(jnp.arange(n, dtype=jnp.int32), TOP_K)
    row_tok = jnp.zeros((n_rows,), jnp.int32).at[slot].set(tok)
    row_w = jnp.zeros((n_rows,), F32).at[slot].set(wts.reshape(-1))
    xb = _gather_rows(xf, row_tok)
    yb = _moe_ffn(xb, row_w.reshape(n_rows, 1), block_e, n_used, w_gate, w_up, w_down)
    return _combine(x1, yb, slot.reshape(n, TOP_K))


def _layer(xp, xs, conv_s, c_s, n_s, m_s, cache_k, cache_v, page_table, lambda_init, p):
    (norm_mix, w_in, b_gate, conv_w, conv_b, q_norm, k_norm, da_lambda, da_subln,
     m_outnorm, w_out, norm_ffn, w_group, b_group, w_router, b_router, w_gate, w_up, w_down) = p
    bp, tp, d = xp.shape
    bs, ts, _ = xs.shape
    n_p, n_s_rows = bp * tp, bs * ts
    heads_m, dk, dv = c_s.shape[1:]
    da_heads = cache_k.shape[2]
    da_w = da_heads * LANES
    m_w = heads_m * dv
    conv_c = conv_w.shape[1]
    assert da_w == m_w == conv_c, "the six projection regions are assumed equally wide"
    region = da_w
    n_reg = 6
    lp = da_lambda.astype(F32)
    lam = jnp.exp(jnp.sum(lp[0] * lp[1])) - jnp.exp(jnp.sum(lp[2] * lp[3])) + lambda_init
    out_scale = 1.0 - lambda_init

    x = jnp.concatenate([xp.reshape(n_p, d), xs.reshape(n_s_rows, d)], axis=0)
    w_main = w_in[:, :n_reg * region].astype(BF16)
    w_g = jnp.pad(w_in[:, n_reg * region:], ((0, 0), (0, LANES - 2 * heads_m))).astype(BF16)
    proj, gates = _input_projection(x, norm_mix, w_main, w_g, region)
    proj_s = proj[:, n_p:]

    k_f32, q2, k2, v2 = _qk_prep(proj, q_norm, k_norm, tp)
    o_da_p = _prompt_attention(q2, k2, v2, bp, tp, lam, da_subln, out_scale)
    sample3 = lambda a: a[n_p:].reshape(bs, ts, 2 * da_w)
    o_da_s = _paged_attention(sample3(q2), sample3(k2), sample3(v2), cache_k, cache_v, page_table,
                              lam, da_subln, out_scale)

    q_scale = dk ** -0.5
    zeros_state = jnp.zeros((bp, SUBLANES, conv_c), F32)
    mq_p, mk_p = _conv_silu(proj, 3, bp, tp, zeros_state, conv_w, conv_b, q_scale)
    assert ts <= SUBLANES
    qk_s = proj_s[3].reshape(bs, ts, conv_c)
    xs16 = jnp.concatenate([jnp.zeros((bs, 2 * SUBLANES - ts - (CONV_W - 1), conv_c), F32), conv_s, qk_s], axis=1)
    xs16 = xs16.reshape(1, bs * 2 * SUBLANES, conv_c)
    mq_s, mk_s = _conv_silu(xs16, 0, bs, 2 * SUBLANES, jnp.zeros((bs, SUBLANES, conv_c), F32),
                            conv_w, conv_b, q_scale)
    pad8 = lambda a: jnp.pad(a, ((0, 0), (0, SUBLANES - ts), (0, 0)))
    mq_s = pad8(mq_s[:, 2 * SUBLANES - ts:]).astype(F32)
    mk_s = pad8(mk_s[:, 2 * SUBLANES - ts:]).astype(F32)

    def region_spec(r, nb_rows):
        return lambda rows, nc: pl.BlockSpec((1, rows, region), lambda bi, ci: (r, bi * nc + ci, 0))

    zc = jnp.zeros((bp, heads_m, dk, dv), F32)
    o_m_p, c_p, n_pp, m_p = _mlstm(mq_p, mk_p, proj, region_spec(4, n_p), proj, region_spec(5, n_p),
                                   gates[:n_p].reshape(bp, tp, LANES), b_gate, m_outnorm,
                                   zc, jnp.zeros((bp, heads_m, dk), F32), jnp.zeros((bp, heads_m), F32), M_CHUNK)
    seq_spec = lambda rows, nc: pl.BlockSpec((1, rows, region), lambda bi, ci: (bi, ci, 0))
    v_s8 = pad8(proj_s[4].reshape(bs, ts, region))
    og_s8 = pad8(proj_s[5].reshape(bs, ts, region))
    g_s8 = pad8(gates[n_p:].reshape(bs, ts, LANES))
    o_m_s, c_new, n_new, m_new = _mlstm(mq_s, mk_s, v_s8, seq_spec, og_s8, seq_spec, g_s8, b_gate, m_outnorm,
                                        c_s, n_s, m_s, ts)

    o_da = jnp.concatenate([o_da_p, o_da_s.reshape(n_s_rows, da_w).astype(BF16)], axis=0)
    o_m = jnp.concatenate([o_m_p.reshape(n_p, m_w), o_m_s[:, :ts].reshape(n_s_rows, m_w).astype(BF16)], axis=0)
    x1 = _output_projection(o_da, o_m, w_out.astype(BF16), x)
    y = _moe(x1, norm_ffn, w_group, b_group, w_router, b_router, w_gate, w_up, w_down)

    k4 = lambda a, b_, t_: a.reshape(b_, t_, da_heads, LANES)
    conv_tail = lambda a, b_, t_: a.reshape(b_, t_, conv_c)[:, t_ - (CONV_W - 1):]
    conv_new_s = jnp.concatenate([conv_s, qk_s], axis=1)[:, ts:]
    outs_p = (k4(k_f32[:n_p], bp, tp), k4(proj[2, :n_p], bp, tp), c_p, n_pp, m_p, conv_tail(proj[3, :n_p], bp, tp))
    outs_s = (k4(k_f32[n_p:], bs, ts), k4(proj_s[2], bs, ts), c_new, n_new, m_new, conv_new_s)
    return y[:n_p].reshape(bp, tp, d), y[n_p:].reshape(bs, ts, d), outs_p, outs_s


def kernel(x_prompt, x_sample, cache_k, cache_v, state_C, state_n, state_m, state_conv, page_table, norm_mix, w_in, b_gate, conv_w, conv_b, q_norm, k_norm, da_lambda, da_subln, m_outnorm, w_out, norm_ffn, w_group, b_group, w_router, b_router, w_gate, w_up, w_down):
    depth = w_in.shape[0]
    yp, ys = x_prompt, x_sample
    outs = [[] for _ in range(12)]
    for l in range(depth):
        lambda_init = 0.8 - 0.6 * math.exp(-0.3 * l)
        p = (norm_mix[l], w_in[l], b_gate[l], conv_w[l], conv_b[l], q_norm[l], k_norm[l], da_lambda[l],
             da_subln[l], m_outnorm[l], w_out[l], norm_ffn[l], w_group[l], b_group[l], w_router[l],
             b_router[l], w_gate[l], w_up[l], w_down[l])
        yp, ys, outs_p, outs_s = _layer(yp, ys, state_conv[l], state_C[l], state_n[l], state_m[l],
                                        cache_k[l], cache_v[l], page_table, lambda_init, p)
        for i, a in enumerate(outs_p + outs_s):
            outs[i].append(a)
    return (yp, ys) + tuple(jnp.stack(o) for o in outs)
```

```python
import functools
import math

import numpy as np
import jax
import jax.numpy as jnp
from jax import lax
from jax.experimental import pallas as pl
from jax.experimental.pallas import tpu as pltpu

F32 = jnp.float32
BF16 = jnp.bfloat16

EPS = 1e-6
ALIBI_MAX = 8.0
CONV_W = 4
TOP_K = 2
M_CHUNK = 128
NEG = -1e30

LANES = 128
SUBLANES = 8
VMEM_BYTES_V7X = 64 * 1024 * 1024
VMEM_LIMIT = VMEM_BYTES_V7X - 8 * 1024 * 1024


def _row_tile(n, cap):
    t = cap
    while t > SUBLANES and n % t:
        t //= 2
    assert n % t == 0, (n, t)
    return t


def _params(sem, vmem=None):
    return pltpu.CompilerParams(dimension_semantics=sem, vmem_limit_bytes=vmem or VMEM_LIMIT)


def _two_group_specs(tm, width, nb_first, n_col=1, **spec_kwargs):
    first = pl.BlockSpec((tm, width), lambda i, j: (jnp.minimum(i, nb_first - 1),
                                                    jnp.where(i < nb_first, j, n_col - 1) if n_col > 1 else 0),
                         **spec_kwargs)
    second = pl.BlockSpec((tm, width), lambda i, j: (jnp.maximum(i - nb_first, 0),
                                                     jnp.where(i >= nb_first, j, 0) if n_col > 1 else 0),
                          **spec_kwargs)
    return first, second


def _proj_kernel(xa_ref, xb_ref, g_ref, w_ref, wg_ref, o_ref, og_ref, xn_ref, *, nb_first):
    def normalise(x_ref):
        x = x_ref[...]
        ms = jnp.mean(x * x, axis=-1, keepdims=True)
        xn = ((x * lax.rsqrt(ms + EPS)) * g_ref[...]).astype(BF16)
        xn_ref[...] = xn
        og_ref[...] = jnp.dot(xn, wg_ref[...], preferred_element_type=F32)

    i = pl.program_id(0)
    first_col = pl.program_id(1) == 0
    pl.when(first_col & (i < nb_first))(lambda: normalise(xa_ref))
    pl.when(first_col & (i >= nb_first))(lambda: normalise(xb_ref))
    o_ref[0] = jnp.dot(xn_ref[...], w_ref[...], preferred_element_type=F32)


def _input_projection(xa, xb, g, w, wg, region, n_reg):
    d = xa.shape[1]
    n = xa.shape[0] + xb.shape[0]
    e = n_reg * region
    tm = _row_tile(math.gcd(xa.shape[0], xb.shape[0]), 512)
    nb_first = xa.shape[0] // tm
    tn = min(1024, region)
    per = region // tn
    return pl.pallas_call(
        functools.partial(_proj_kernel, nb_first=nb_first),
        grid=(n // tm, e // tn),
        in_specs=[
            *_two_group_specs(tm, d, nb_first, pipeline_mode=pl.Buffered(1)),
            pl.BlockSpec((1, d), lambda i, j: (0, 0)),
            pl.BlockSpec((d, tn), lambda i, j: (0, j)),
            pl.BlockSpec((d, LANES), lambda i, j: (0, 0)),
        ],
        out_specs=[
            pl.BlockSpec((1, tm, tn), lambda i, j: (j // per, i, j % per)),
            pl.BlockSpec((tm, LANES), lambda i, j: (i, 0)),
        ],
        out_shape=[
            jax.ShapeDtypeStruct((n_reg, n, region), F32),
            jax.ShapeDtypeStruct((n, LANES), F32),
        ],
        scratch_shapes=[pltpu.VMEM((tm, d), BF16)],
        compiler_params=_params(("arbitrary", "arbitrary")),
        name="input_projection",
    )(xa, xb, g.reshape(1, d), w, wg)


SLOPE_TERMS = 4
FEAT = 2 * SLOPE_TERMS
POS_SPLIT = 64
LOG2E = math.log2(math.e)


def _alibi_slopes(n):
    return np.array([2.0 ** (-ALIBI_MAX * (h + 1) / n) for h in range(n)], np.float32)


def _slope_terms(heads):
    import ml_dtypes
    rest = _alibi_slopes(heads).astype(np.float64) * LOG2E
    terms = []
    for _ in range(SLOPE_TERMS):
        t = rest.astype(ml_dtypes.bfloat16).astype(np.float64)
        terms.append(t)
        rest = rest - t
    return np.stack(terms, axis=-1).astype(np.float32)


def _qkprep_kernel(q_ref, k_ref, v_ref, gq_ref, gk_ref, grp_ref, sf_ref, ko_ref, q2_ref, k2_ref, v2_ref,
                   *, heads, half, q_scale, tm, t_seq):
    grp = grp_ref[...]
    lane = lax.broadcasted_iota(jnp.int32, (1, LANES), 1)
    lo_half = lane < half
    fl = lane & (half - 1)
    pos = (pl.program_id(0) * tm) % t_seq + lax.broadcasted_iota(jnp.int32, (tm, 1), 0)
    pos_hi = (pos - (pos & (POS_SPLIT - 1))).astype(F32)
    pos_lo = (pos & (POS_SPLIT - 1)).astype(F32)
    kfeat = jnp.where(fl < FEAT, jnp.where((fl & 1) == 0, pos_hi, pos_lo), 0.0)
    ones_col = jnp.broadcast_to(jnp.where(lane == 0, 1.0, 0.0), (tm, LANES)).astype(BF16)

    def half_norm(x, g):
        sq = x * x
        hi = sq.astype(BF16)
        lo = (sq - hi.astype(F32)).astype(BF16)
        s = jnp.dot(hi, grp, preferred_element_type=F32) + jnp.dot(lo, grp, preferred_element_type=F32)
        return (x * lax.rsqrt(s * (1.0 / half) + EPS)) * g

    for h in range(heads):
        sl = slice(h * LANES, (h + 1) * LANES)
        m0 = slice(2 * h * LANES, (2 * h + 1) * LANES)
        m1 = slice((2 * h + 1) * LANES, (2 * h + 2) * LANES)
        qn = half_norm(q_ref[0, :, sl], gq_ref[...]) * q_scale
        sf = sf_ref[h:h + 1, :]
        q2_ref[:, m0] = jnp.where(lo_half, qn, sf).astype(BF16)
        q2_ref[:, m1] = jnp.where(lo_half, sf, qn).astype(BF16)
        kn = half_norm(k_ref[0, :, sl], gk_ref[...])
        ko_ref[:, sl] = kn
        k2_ref[:, m0] = jnp.where(lo_half, kn, kfeat).astype(BF16)
        k2_ref[:, m1] = jnp.where(lo_half, kfeat, kn).astype(BF16)
        v2_ref[:, m0] = v_ref[0, :, sl].astype(BF16)
        v2_ref[:, m1] = ones_col


def _qk_prep(proj, q_norm, k_norm, t_seq):
    _, n, w = proj.shape
    heads = w // LANES
    half = q_norm.shape[0]
    assert 2 * half == LANES and FEAT <= half
    tm = _row_tile(math.gcd(n, t_seq), 256)
    assert t_seq <= POS_SPLIT * 256, "key positions must split into two bf16-exact factors"
    lane = np.arange(LANES)
    grp = jnp.asarray((lane[:, None] // half == lane[None, :] // half).astype(np.float32), BF16)
    gq = jnp.concatenate([q_norm, q_norm]).reshape(1, LANES)
    gk = jnp.concatenate([k_norm, k_norm]).reshape(1, LANES)
    terms = _slope_terms(heads)
    fl = lane % half
    sf = jnp.asarray(np.where(fl[None, :] < FEAT, terms[:, np.minimum(fl // 2, SLOPE_TERMS - 1)], 0.0))
    row = lambda r: pl.BlockSpec((1, tm, w), lambda i: (r, i, 0))
    full = lambda shape: pl.BlockSpec(shape, lambda i: (0, 0))
    out1 = pl.BlockSpec((tm, w), lambda i: (i, 0))
    out2 = pl.BlockSpec((tm, 2 * w), lambda i: (i, 0))
    return pl.pallas_call(
        functools.partial(_qkprep_kernel, heads=heads, half=half, q_scale=half ** -0.5 * LOG2E,
                          tm=tm, t_seq=t_seq),
        grid=(n // tm,),
        in_specs=[row(0), row(1), row(2), full((1, LANES)), full((1, LANES)), full((LANES, LANES)),
                  full((heads, LANES))],
        out_specs=[out1, out2, out2, out2],
        out_shape=[
            jax.ShapeDtypeStruct((n, w), F32),
            jax.ShapeDtypeStruct((n, 2 * w), BF16),
            jax.ShapeDtypeStruct((n, 2 * w), BF16),
            jax.ShapeDtypeStruct((n, 2 * w), BF16),
        ],
        compiler_params=_params(("arbitrary",)),
        name="qk_prep",
    )(proj, proj, proj, gq, gk, grp, sf)


def _subln(o, g, out_scale):
    ms = jnp.mean(o * o, axis=-1, keepdims=True)
    return ((o * lax.rsqrt(ms + EPS)) * g) * out_scale


def _attn_kernel(lam_ref, q_ref, k_ref, v_ref, gs_ref, o_ref, m0_ref, a0_ref, m1_ref, a1_ref,
                 *, tq, out_scale):
    i = pl.program_id(2)
    q = q_ref[...]
    qs = (q[:, :LANES], q[:, LANES:])
    states = ((m0_ref, a0_ref), (m1_ref, a1_ref))
    for m_ref, a_ref in states:
        m_ref[...] = jnp.full(m_ref.shape, NEG, F32)
        a_ref[...] = jnp.zeros(a_ref.shape, F32)
    causal = lax.broadcasted_iota(jnp.int32, (tq, tq), 0) >= lax.broadcasted_iota(jnp.int32, (tq, tq), 1)

    def scores(j):
        kt = k_ref[pl.ds(pl.multiple_of(j * tq, tq), tq), :]
        return tuple(lax.dot_general(qs[c], kt[:, c * LANES:(c + 1) * LANES], (((1,), (1,)), ((), ())),
                                     preferred_element_type=F32) for c in range(2))

    def accumulate(j, s_maps, masked):
        vt = v_ref[pl.ds(pl.multiple_of(j * tq, tq), tq), :]
        for s, (m_ref, a_ref) in zip(s_maps, states):
            if masked:
                s = jnp.where(causal, s, NEG)
            m_old = m_ref[...]
            m_new = jnp.maximum(m_old, jnp.max(s, axis=-1, keepdims=True))
            p = jnp.exp2(s - m_new)
            a_ref[...] = jnp.exp2(m_old - m_new) * a_ref[...] + jnp.dot(p.astype(BF16), vt,
                                                                         preferred_element_type=F32)
            m_ref[...] = m_new

    def body(j, s_maps):
        s_next = scores(j + 1)
        accumulate(j, s_maps, False)
        return s_next

    accumulate(i, lax.fori_loop(0, i, body, scores(0)), True)
    a0 = a0_ref[...]
    a1 = a1_ref[...]
    o = a0[:, :LANES] / a0[:, LANES:LANES + 1] - lam_ref[0] * (a1[:, :LANES] / a1[:, LANES:LANES + 1])
    o_ref[...] = _subln(o, gs_ref[...], out_scale).astype(o_ref.dtype)


def _prompt_attention(q2, k2, v2, n_batch, t, lam, subln, out_scale):
    heads = q2.shape[1] // (2 * LANES)
    tq = _row_tile(t, 512)
    nq = t // tq
    kv = pl.BlockSpec((t, 2 * LANES), lambda bi, hi, i, lm: (bi, hi))
    return pl.pallas_call(
        functools.partial(_attn_kernel, tq=tq, out_scale=out_scale),
        grid_spec=pltpu.PrefetchScalarGridSpec(
            num_scalar_prefetch=1,
            grid=(n_batch, heads, nq),
            in_specs=[pl.BlockSpec((tq, 2 * LANES), lambda bi, hi, i, lm: (bi * nq + i, hi)), kv, kv,
                      pl.BlockSpec((1, LANES), lambda bi, hi, i, lm: (0, 0))],
            out_specs=pl.BlockSpec((tq, LANES), lambda bi, hi, i, lm: (bi * nq + i, hi)),
            scratch_shapes=[pltpu.VMEM((tq, 1), F32), pltpu.VMEM((tq, 2 * LANES), F32)] * 2,
        ),
        out_shape=jax.ShapeDtypeStruct((n_batch * t, heads * LANES), BF16),
        compiler_params=_params(("arbitrary", "arbitrary", "arbitrary")),
        name="prompt_diff_attention",
    )(lam.reshape(1), q2, k2, v2, subln.reshape(1, LANES))


QSLOTS = 4
NEW_SLOTS = 8


def _paged_attn_kernel(pt_ref, lam_ref, *refs, pages, page, out_scale):
    kp_refs = refs[:pages]
    vp_refs = refs[pages:2 * pages]
    (q_ref, kn_ref, vn_ref, bias_ref, biasn_ref, slope_ref, gs_ref, o_ref, m_ref, l_ref, a_ref) = refs[2 * pages:]
    g = pl.program_id(1)
    q = q_ref[0]
    slope_c = slope_ref[...]
    nt = (((1,), (1,)), ((), ()))

    def update(tiles, first):
        m_old = jnp.full(m_ref.shape, NEG, F32) if first else m_ref[...]
        m_new = m_old
        for s, shift_c, _ in tiles:
            m_new = jnp.maximum(m_new, jnp.max(s, axis=-1, keepdims=True) + shift_c)
        psum = None
        pv = None
        for s, shift_c, v_tile in tiles:
            p = jnp.exp2(s - (m_new - shift_c))
            ps = jnp.sum(p, axis=-1, keepdims=True)
            d = jnp.dot(p.astype(BF16), v_tile, preferred_element_type=F32)
            psum = ps if psum is None else psum + ps
            pv = d if pv is None else pv + d
        if first:
            l_ref[...] = psum
            a_ref[...] = pv
        else:
            alpha = jnp.exp2(m_old - m_new)
            l_ref[...] = alpha * l_ref[...] + psum
            a_ref[...] = alpha * a_ref[...] + pv
        m_ref[...] = m_new

    @pl.when(g == 0)
    def _():
        s = lax.dot_general(q, kn_ref[0], nt, preferred_element_type=F32) + biasn_ref[...]
        update([(s, jnp.zeros_like(slope_c), vn_ref[0])], True)

    tiles = []
    for p in range(pages):
        kt = kp_refs[p][0].astype(BF16)
        s = lax.dot_general(q, kt, nt, preferred_element_type=F32) + bias_ref[...]
        first_pos = ((g * pages + p) * page).astype(F32)
        tiles.append((s, slope_c * first_pos, vp_refs[p][0].astype(BF16)))
    update(tiles, False)

    @pl.when(g == pl.num_programs(1) - 1)
    def _():
        a = a_ref[...] / l_ref[...]
        half_rows = a.shape[0] // 2
        o = a[:half_rows] - lam_ref[0] * a[half_rows:]
        o_ref[0] = _subln(o, gs_ref[...], out_scale)


def _paged_attention(q2, k2, v2, cache_k, cache_v, page_table, lam, subln, out_scale):
    b, tq, _ = q2.shape
    n_pool, page, heads, _ = cache_k.shape
    half = LANES // 2
    n_pages = page_table.shape[1]
    past_len = n_pages * page
    pages = math.gcd(n_pages, 4)
    rows = 2 * heads * QSLOTS
    assert tq <= QSLOTS and tq <= NEW_SLOTS
    q5 = q2.reshape(b, tq, heads, 2, LANES)
    zero = jnp.zeros((b, tq, heads, half), BF16)
    qmaps = jnp.stack([jnp.concatenate([q5[:, :, :, 0, :half], zero], axis=-1),
                       jnp.concatenate([zero, q5[:, :, :, 1, half:]], axis=-1)], axis=1)
    qrows = jnp.pad(qmaps.transpose(0, 1, 3, 2, 4), ((0, 0), (0, 0), (0, 0), (0, QSLOTS - tq), (0, 0)))
    qrows = qrows.reshape(b, rows, LANES)
    k5 = k2.reshape(b, tq, heads, 2, LANES)
    kn = jnp.concatenate([k5[:, :, :, 0, :half], k5[:, :, :, 1, half:]], axis=-1)
    vn = v2.reshape(b, tq, heads, 2, LANES)[:, :, :, 0, :]
    pad_t = ((0, 0), (0, NEW_SLOTS - tq), (0, 0), (0, 0))
    kn = jnp.pad(kn, pad_t).reshape(b, NEW_SLOTS * heads, LANES)
    vn = jnp.pad(vn, pad_t).reshape(b, NEW_SLOTS * heads, LANES)
    slope2 = _slope_terms(heads).astype(np.float64).sum(axis=-1)
    r = np.arange(rows)
    row_head = (r // QSLOTS) % heads
    row_q = r % QSLOTS

    def bias_matrix(n_tok, key_pos0):
        col = np.arange(n_tok * heads)
        col_tok, col_head = col // heads, col % heads
        dist = (past_len + row_q)[:, None] - (key_pos0 + col_tok)[None, :]
        ok = (row_head[:, None] == col_head[None, :]) & (dist >= 0)
        return jnp.asarray(np.where(ok, -slope2[row_head][:, None] * dist, NEG).astype(np.float32))

    bias_page = bias_matrix(page, 0)
    new_tok = np.arange(NEW_SLOTS * heads) // heads
    bias_new = jnp.where(jnp.asarray(new_tok < tq)[None, :], bias_matrix(NEW_SLOTS, past_len), NEG)
    slope_col = jnp.asarray(slope2[row_head].astype(np.float32).reshape(rows, 1))
    ck = cache_k.reshape(n_pool, page * heads, LANES)
    cv = cache_v.reshape(n_pool, page * heads, LANES)

    def page_spec(p):
        return pl.BlockSpec((1, page * heads, LANES),
                            lambda bi, g, pt, lm: (pt[bi * n_pages + g * pages + p], 0, 0))

    per_b = lambda shape: pl.BlockSpec(shape, lambda bi, g, pt, lm: (bi,) + (0,) * (len(shape) - 1))
    const = lambda shape: pl.BlockSpec(shape, lambda bi, g, pt, lm: (0,) * len(shape))
    out = pl.pallas_call(
        functools.partial(_paged_attn_kernel, pages=pages, page=page, out_scale=out_scale),
        grid_spec=pltpu.PrefetchScalarGridSpec(
            num_scalar_prefetch=2,
            grid=(b, n_pages // pages),
            in_specs=[page_spec(p) for p in range(pages)] * 2 + [
                per_b((1, rows, LANES)), per_b((1, NEW_SLOTS * heads, LANES)), per_b((1, NEW_SLOTS * heads, LANES)),
                const((rows, page * heads)), const((rows, NEW_SLOTS * heads)), const((rows, 1)),
                const((1, LANES))],
            out_specs=per_b((1, rows // 2, LANES)),
            scratch_shapes=[pltpu.VMEM((rows, 1), F32), pltpu.VMEM((rows, 1), F32),
                            pltpu.VMEM((rows, LANES), F32)],
        ),
        out_shape=jax.ShapeDtypeStruct((b, rows // 2, LANES), F32),
        compiler_params=_params(("arbitrary", "arbitrary")),
        name="paged_diff_attention",
    )(page_table.reshape(-1), lam.reshape(1), *([ck] * pages), *([cv] * pages),
      qrows, kn, vn, bias_page, bias_new, slope_col, subln.reshape(1, LANES))
    return out.reshape(b, heads, QSLOTS, LANES)[:, :, :tq].transpose(0, 2, 1, 3).reshape(b, tq, heads * LANES)


def _conv_kernel(x_ref, st_ref, w_ref, b_ref, qo_ref, ko_ref, buf_ref, *, tt, q_scale, carry):
    @pl.when(pl.program_id(1) == 0)
    def _():
        buf_ref[0:SUBLANES, :] = st_ref[0]

    x = x_ref[0]
    buf_ref[SUBLANES:SUBLANES + tt, :] = x
    acc = b_ref[...] + w_ref[CONV_W - 1:CONV_W, :] * x
    for k in range(1, CONV_W):
        acc = acc + w_ref[CONV_W - 1 - k:CONV_W - k, :] * buf_ref[SUBLANES - k:SUBLANES - k + tt, :]
    u = acc * jax.nn.sigmoid(acc)
    c = u.shape[1] // 2
    qo_ref[0] = (u[:, :c] * q_scale).astype(qo_ref.dtype)
    ko_ref[0] = u[:, c:].astype(ko_ref.dtype)
    if carry:
        buf_ref[0:SUBLANES, :] = buf_ref[tt:tt + SUBLANES, :]


def _conv_silu(x, region, n_batch, t, state8, conv_w, conv_b, q_scale):
    c = x.shape[2]
    tt = _row_tile(t, 256)
    nt = t // tt
    w8 = jnp.pad(conv_w, ((0, SUBLANES - CONV_W), (0, 0)))
    out = pl.BlockSpec((1, tt, c // 2), lambda bi, i: (bi, i, 0))
    return pl.pallas_call(
        functools.partial(_conv_kernel, tt=tt, q_scale=q_scale, carry=nt > 1),
        grid=(n_batch, nt),
        in_specs=[
            pl.BlockSpec((1, tt, c), lambda bi, i: (region, bi * nt + i, 0)),
            pl.BlockSpec((1, SUBLANES, c), lambda bi, i: (bi, 0, 0)),
            pl.BlockSpec((SUBLANES, c), lambda bi, i: (0, 0)),
            pl.BlockSpec((1, c), lambda bi, i: (0, 0)),
        ],
        out_specs=[out, out],
        out_shape=[jax.ShapeDtypeStruct((n_batch, t, c // 2), BF16)] * 2,
        scratch_shapes=[pltpu.VMEM((tt + SUBLANES, c), F32)],
        compiler_params=_params(("arbitrary", "arbitrary")),
        name="mlstm_conv_silu",
    )(x, state8, w8, conv_b.reshape(1, c))


def _mlstm_kernel(q_ref, k_ref, v_ref, og_ref, g_ref, bg_ref, on_ref, c0_ref, n0_ref, m0_ref,
                  o_ref, c_ref, n_ref, m_ref, cs_ref, ns_ref, ms_ref, *, heads, dk, dv, rows, chunk, t_valid):
    L = chunk
    ci = pl.program_id(1)

    @pl.when(ci == 0)
    def _():
        cs_ref[...] = c0_ref[0]
        ns_ref[...] = n0_ref[0]
        ms_ref[...] = m0_ref[0]

    def pad_rows(a):
        if rows == L:
            return a
        return jnp.concatenate([a, jnp.zeros((L - rows, a.shape[1]), a.dtype)], axis=0)

    gates = pad_rows(g_ref[0] + bg_ref[...])
    lf_all = jnp.minimum(gates, 0.0) - jnp.log1p(jnp.exp(-jnp.abs(gates)))
    q_all = pad_rows(q_ref[0]).astype(BF16)
    k_all = pad_rows(k_ref[0]).astype(BF16)
    v_all = pad_rows(v_ref[0]).astype(BF16)
    r_i = lax.broadcasted_iota(jnp.int32, (L, L), 0)
    c_i = lax.broadcasted_iota(jnp.int32, (L, L), 1)
    eye = r_i == c_i
    tril = c_i <= r_i
    valid = lax.broadcasted_iota(jnp.int32, (L, 1), 0) < t_valid

    def to_row(col):
        return jnp.sum(jnp.where(eye, col, 0.0), axis=0, keepdims=True)

    for h in range(heads):
        ig_c = jnp.where(valid, gates[:, h:h + 1], NEG)
        lf_c = jnp.where(valid, lf_all[:, heads + h:heads + h + 1], 0.0)
        ig_r = to_row(ig_c)
        lf_r = to_row(lf_c)
        b_c = jnp.sum(jnp.where(tril, lf_r, 0.0), axis=1, keepdims=True)
        b_r = to_row(b_c)
        m_prev = ms_ref[h:h + 1, 0:1]
        dmat = jnp.where(tril, (b_c - b_r) + ig_r, NEG)
        inter = b_c + m_prev
        mt = jnp.maximum(inter, jnp.max(dmat, axis=1, keepdims=True))
        qh = q_all[:, h * dk:(h + 1) * dk]
        kh = k_all[:, h * dk:(h + 1) * dk]
        vh = v_all[:, h * dv:(h + 1) * dv]
        c_old = cs_ref[h]
        n_old = ns_ref[h:h + 1, :]
        s = lax.dot_general(qh, kh, (((1,), (1,)), ((), ())), preferred_element_type=F32)
        wm = s * jnp.exp(dmat - mt)
        e_in = jnp.exp(inter - mt)
        num = e_in * jnp.dot(qh, c_old.astype(BF16), preferred_element_type=F32) \
            + jnp.dot(wm.astype(BF16), vh, preferred_element_type=F32)
        den = e_in * jnp.sum(qh.astype(F32) * n_old, axis=1, keepdims=True) + jnp.sum(wm, axis=1, keepdims=True)
        hid = num / jnp.maximum(jnp.abs(den), jnp.exp(-mt))
        ms_h = jnp.mean(hid * hid, axis=-1, keepdims=True)
        hn = (hid * lax.rsqrt(ms_h + EPS)) * on_ref[...]
        gate = jax.nn.sigmoid(og_ref[0, :, h * dv:(h + 1) * dv])
        o_ref[0, :, h * dv:(h + 1) * dv] = (gate * hn[:rows]).astype(o_ref.dtype)
        b_last = jnp.sum(lf_r, axis=1, keepdims=True)
        g_c = (b_last - b_c) + ig_c
        g_r = (b_last - b_r) + ig_r
        m_new = jnp.maximum(b_last + m_prev, jnp.max(g_r, axis=1, keepdims=True))
        e_old = jnp.exp((b_last + m_prev) - m_new)
        kw = kh.astype(F32) * jnp.exp(g_c - m_new)
        cs_ref[h] = e_old * c_old + lax.dot_general(kw.astype(BF16), vh, (((0,), (0,)), ((), ())),
                                                    preferred_element_type=F32)
        ns_ref[h:h + 1, :] = e_old * n_old + jnp.sum(kw, axis=0, keepdims=True)
        ms_ref[h:h + 1, :] = jnp.broadcast_to(m_new, (1, LANES))

    @pl.when(ci == pl.num_programs(1) - 1)
    def _():
        c_ref[0] = cs_ref[...]
        n_ref[0] = ns_ref[...]
        m_ref[0] = ms_ref[...]


def _mlstm(q, k, v_src, v_spec, og_src, og_spec, gates, b_gate, outnorm, c0, n0, m0, t_valid):
    b, t, _ = q.shape
    heads, dk, dv = c0.shape[1:]
    rows = min(t, M_CHUNK)
    nc = t // rows
    bg = jnp.pad(b_gate, (0, LANES - b_gate.shape[0])).reshape(1, LANES)
    m0b = jnp.broadcast_to(m0[:, :, None], (b, heads, LANES))
    blk = lambda w: pl.BlockSpec((1, rows, w), lambda bi, ci: (bi, ci, 0))
    per_b = lambda shape: pl.BlockSpec(shape, lambda bi, ci: (bi,) + (0,) * (len(shape) - 1))
    const = lambda shape: pl.BlockSpec(shape, lambda bi, ci: (0,) * len(shape))
    chunk = M_CHUNK if rows == M_CHUNK else 2 * SUBLANES
    kernel = functools.partial(_mlstm_kernel, heads=heads, dk=dk, dv=dv, rows=rows, chunk=chunk, t_valid=t_valid)
    o, c, n, m = pl.pallas_call(
        kernel,
        grid=(b, nc),
        in_specs=[blk(heads * dk), blk(heads * dk), v_spec(rows, nc), og_spec(rows, nc), blk(LANES),
                  const((1, LANES)), const((1, dv)), per_b((1, heads, dk, dv)), per_b((1, heads, dk)),
                  per_b((1, heads, LANES))],
        out_specs=[blk(heads * dv), per_b((1, heads, dk, dv)), per_b((1, heads, dk)), per_b((1, heads, LANES))],
        out_shape=[jax.ShapeDtypeStruct((b, t, heads * dv), BF16 if rows == M_CHUNK else F32),
                   jax.ShapeDtypeStruct((b, heads, dk, dv), F32),
                   jax.ShapeDtypeStruct((b, heads, dk), F32),
                   jax.ShapeDtypeStruct((b, heads, LANES), F32)],
        scratch_shapes=[pltpu.VMEM((heads, dk, dv), F32), pltpu.VMEM((heads, dk), F32),
                        pltpu.VMEM((heads, LANES), F32)],
        compiler_params=_params(("arbitrary", "arbitrary")),
        name="mlstm_chunkwise",
    )(q, k, v_src, og_src, gates, bg, outnorm.reshape(1, dv), c0, n0, m0b)
    return o, c, n, m[:, :, 0]


def _outproj_kernel(a_ref, b_ref, wa_ref, wb_ref, xa_ref, xb_ref, o_ref, *, nb_first):
    mixed = jnp.dot(a_ref[...], wa_ref[...], preferred_element_type=F32) \
        + jnp.dot(b_ref[...], wb_ref[...], preferred_element_type=F32)
    i = pl.program_id(0)

    @pl.when(i < nb_first)
    def _():
        o_ref[...] = xa_ref[...] + mixed

    @pl.when(i >= nb_first)
    def _():
        o_ref[...] = xb_ref[...] + mixed


def _output_projection(a, bmix, w, xa, xb):
    n, wa = a.shape
    wb = bmix.shape[1]
    d = w.shape[1]
    tm = _row_tile(math.gcd(xa.shape[0], xb.shape[0]), 512)
    nb_first = xa.shape[0] // tm
    tn = min(1024, d)
    return pl.pallas_call(
        functools.partial(_outproj_kernel, nb_first=nb_first),
        grid=(n // tm, d // tn),
        in_specs=[
            pl.BlockSpec((tm, wa), lambda i, j: (i, 0)),
            pl.BlockSpec((tm, wb), lambda i, j: (i, 0)),
            pl.BlockSpec((wa, tn), lambda i, j: (0, j)),
            pl.BlockSpec((wb, tn), lambda i, j: (wa // wb, j)),
            *_two_group_specs(tm, tn, nb_first, n_col=d // tn),
        ],
        out_specs=pl.BlockSpec((tm, tn), lambda i, j: (i, j)),
        out_shape=jax.ShapeDtypeStruct((n, d), F32),
        compiler_params=_params(("arbitrary", "arbitrary")),
        name="output_projection",
    )(a, bmix, w, w, xa, xb)


def _router_kernel(x_ref, g_ref, whi_ref, wlo_ref, b_ref, xf_ref, lg_ref):
    x = x_ref[...]
    ms = jnp.mean(x * x, axis=-1, keepdims=True)
    xn = (x * lax.rsqrt(ms + EPS)) * g_ref[...]
    hi = xn.astype(BF16)
    lo = (xn - hi.astype(F32)).astype(BF16)
    xf_ref[...] = xn
    lg_ref[...] = b_ref[...] + jnp.dot(hi, whi_ref[...], preferred_element_type=F32) \
        + (jnp.dot(hi, wlo_ref[...], preferred_element_type=F32)
           + jnp.dot(lo, whi_ref[...], preferred_element_type=F32))


def _ffn_norm_router(x, g, w_route, b_route):
    n, d = x.shape
    tm = _row_tile(n, 256)
    whi = w_route.astype(BF16)
    wlo = (w_route - whi.astype(F32)).astype(BF16)
    const = lambda shape: pl.BlockSpec(shape, lambda i: (0, 0))
    return pl.pallas_call(
        _router_kernel,
        grid=(n // tm,),
        in_specs=[pl.BlockSpec((tm, d), lambda i: (i, 0)), const((1, d)), const((d, LANES)),
                  const((d, LANES)), const((1, LANES))],
        out_specs=[pl.BlockSpec((tm, d), lambda i: (i, 0)), pl.BlockSpec((tm, LANES), lambda i: (i, 0))],
        out_shape=[jax.ShapeDtypeStruct((n, d), F32), jax.ShapeDtypeStruct((n, LANES), F32)],
        compiler_params=_params(("arbitrary",)),
        name="ffn_norm_router",
    )(x, g.reshape(1, d), whi, wlo, b_route.reshape(1, LANES))


MOE_ROWS = 256
MOE_FF_TILE = 256
MOE_OUT_TILE = 1024


def _row_copy(src_ref, src_row, dst_ref, dst_row, sem):
    return pltpu.make_async_copy(src_ref.at[pl.ds(src_row, 1)], dst_ref.at[pl.ds(dst_row, 1)], sem)


DMA_UNROLL = 8


def _gather_rows_kernel(ids_ref, src_ref, o_ref, buf_ref, sem, *, rows):
    def start(r, carry):
        _row_copy(src_ref, ids_ref[0, 0, r], buf_ref, r, sem).start()
        return carry

    def wait(r, carry):
        _row_copy(src_ref, 0, buf_ref, r, sem).wait()
        return carry

    lax.fori_loop(0, rows, start, 0, unroll=DMA_UNROLL)
    lax.fori_loop(0, rows, wait, 0, unroll=DMA_UNROLL)
    o_ref[...] = buf_ref[...].astype(o_ref.dtype)


def _gather_rows(src, ids, out_dtype):
    n_rows = ids.shape[0]
    d = src.shape[1]
    nblk = n_rows // MOE_ROWS
    return pl.pallas_call(
        functools.partial(_gather_rows_kernel, rows=MOE_ROWS),
        grid=(nblk,),
        in_specs=[pl.BlockSpec((1, 1, MOE_ROWS), lambda i: (i, 0, 0), memory_space=pltpu.SMEM),
                  pl.BlockSpec(memory_space=pl.ANY)],
        out_specs=pl.BlockSpec((MOE_ROWS, d), lambda i: (i, 0)),
        out_shape=jax.ShapeDtypeStruct((n_rows, d), out_dtype),
        scratch_shapes=[pltpu.VMEM((MOE_ROWS, d), src.dtype), pltpu.SemaphoreType.DMA(())],
        compiler_params=_params(("arbitrary",)),
        name="moe_gather_rows",
    )(ids.reshape(nblk, 1, MOE_ROWS), src)


COMBINE_ROWS = 128


def _combine_kernel(slots_ref, x_ref, yb_ref, oa_ref, ob_ref, buf_ref, sem, *, tc, nb_first):
    n_copies = TOP_K * tc

    def start(r, carry):
        _row_copy(yb_ref, slots_ref[0, 0, r], buf_ref, r, sem).start()
        return carry

    def wait(r, carry):
        _row_copy(yb_ref, 0, buf_ref, r, sem).wait()
        return carry

    lax.fori_loop(0, n_copies, start, 0, unroll=DMA_UNROLL)
    lax.fori_loop(0, n_copies, wait, 0, unroll=DMA_UNROLL)
    acc = buf_ref[0:tc, :]
    for k in range(1, TOP_K):
        acc = acc + buf_ref[k * tc:(k + 1) * tc, :]
    i = pl.program_id(0)

    @pl.when(i < nb_first)
    def _():
        oa_ref[...] = x_ref[...] + acc

    @pl.when(i >= nb_first)
    def _():
        ob_ref[...] = x_ref[...] + acc


def _combine(x, yb, slot2, n_first):
    n, d = x.shape
    tc = _row_tile(math.gcd(n_first, n - n_first), COMBINE_ROWS)
    nblk = n // tc
    nb_first = n_first // tc
    slots = slot2.reshape(nblk, tc, TOP_K).transpose(0, 2, 1).reshape(nblk, 1, TOP_K * tc)
    return pl.pallas_call(
        functools.partial(_combine_kernel, tc=tc, nb_first=nb_first),
        grid=(nblk,),
        in_specs=[pl.BlockSpec((1, 1, TOP_K * tc), lambda i: (i, 0, 0), memory_space=pltpu.SMEM),
                  pl.BlockSpec((tc, d), lambda i: (i, 0)),
                  pl.BlockSpec(memory_space=pl.ANY)],
        out_specs=[pl.BlockSpec((tc, d), lambda i: (jnp.minimum(i, nb_first - 1), 0)),
                   pl.BlockSpec((tc, d), lambda i: (jnp.maximum(i - nb_first, 0), 0))],
        out_shape=[jax.ShapeDtypeStruct((n_first, d), F32), jax.ShapeDtypeStruct((n - n_first, d), F32)],
        scratch_shapes=[pltpu.VMEM((TOP_K * tc, d), F32), pltpu.SemaphoreType.DMA(())],
        compiler_params=_params(("arbitrary",)),
        name="moe_combine",
    )(slots, x, yb)


def _weights_changed(t, ib_ref, is_ref, be_ref):
    prev = jnp.maximum(t - 1, 0)
    return (t == 0) | (be_ref[ib_ref[t]] != be_ref[ib_ref[prev]]) | (is_ref[t] != is_ref[prev])


def _moe_up_kernel(ib_ref, is_ref, be_ref, ni_ref, x_ref, wg_ref, wu_ref, h_ref, wgb_ref, wub_ref):
    t = pl.program_id(0)

    @pl.when(t >= ni_ref[0])
    def _():
        h_ref[...] = jnp.zeros(h_ref.shape, h_ref.dtype)

    @pl.when(t < ni_ref[0])
    def _():
        @pl.when(_weights_changed(t, ib_ref, is_ref, be_ref))
        def _():
            wgb_ref[...] = wg_ref[0].astype(BF16)
            wub_ref[...] = wu_ref[0].astype(BF16)

        x = x_ref[...]
        gt = jnp.dot(x, wgb_ref[...], preferred_element_type=F32)
        up = jnp.dot(x, wub_ref[...], preferred_element_type=F32)
        h_ref[...] = ((gt * jax.nn.sigmoid(gt)) * up).astype(h_ref.dtype)


def _moe_down_kernel(ib_ref, is_ref, be_ref, ni_ref, h_ref, wd_ref, rw_ref, o_ref, wdb_ref):
    t = pl.program_id(0)

    @pl.when(t >= ni_ref[0])
    def _():
        o_ref[...] = jnp.zeros(o_ref.shape, F32)

    @pl.when(t < ni_ref[0])
    def _():
        @pl.when(_weights_changed(t, ib_ref, is_ref, be_ref))
        def _():
            wdb_ref[...] = wd_ref[0].astype(BF16)

        o_ref[...] = jnp.dot(h_ref[...], wdb_ref[...], preferred_element_type=F32) * rw_ref[...]


def _item_tables(block_e, blocks, bstart, n_used, nblk, nsplit):
    b = jnp.arange(nblk, dtype=jnp.int32)
    s = jnp.arange(nsplit, dtype=jnp.int32)
    e = block_e
    pos_live = nsplit * bstart[e][:, None] + s[None, :] * blocks[e][:, None] + (b - bstart[e])[:, None]
    pos_idle = b[:, None] * nsplit + s[None, :]
    pos = jnp.where((b < n_used[0])[:, None], pos_live, pos_idle).reshape(-1)
    item_b = jnp.zeros((nblk * nsplit,), jnp.int32).at[pos].set(jnp.repeat(b, nsplit))
    item_s = jnp.zeros((nblk * nsplit,), jnp.int32).at[pos].set(jnp.tile(s, nblk))
    return item_b, item_s, (n_used * nsplit).astype(jnp.int32)


def _moe_ffn(xb, row_w, block_e, blocks, bstart, n_used, w_gate, w_up, w_down):
    rows, d = xb.shape
    ff = w_gate.shape[2]
    tf = min(MOE_FF_TILE, ff)
    tn = min(MOE_OUT_TILE, d)
    nblk = rows // MOE_ROWS

    def live(t, ni):
        return jnp.minimum(t, ni[0] - 1)

    def run(kernel, name, nsplit, in_specs, out_spec, out_shape, scratch, args):
        item_b, item_s, n_items = _item_tables(block_e, blocks, bstart, n_used, nblk, nsplit)
        return pl.pallas_call(
            kernel,
            grid_spec=pltpu.PrefetchScalarGridSpec(
                num_scalar_prefetch=4, grid=(nblk * nsplit,), in_specs=in_specs, out_specs=out_spec,
                scratch_shapes=scratch),
            out_shape=out_shape,
            compiler_params=_params(("arbitrary",)),
            name=name,
        )(item_b, item_s, block_e, n_items, *args)

    rows_of = lambda w: pl.BlockSpec((MOE_ROWS, w), lambda t, ib, s, be, ni: (ib[live(t, ni)], 0))
    w_cols = lambda k, w: pl.BlockSpec(
        (1, k, w), lambda t, ib, s, be, ni: (be[ib[live(t, ni)]], 0, s[live(t, ni)]))
    own = lambda w: pl.BlockSpec((MOE_ROWS, w), lambda t, ib, s, be, ni: (ib[t], s[t]))
    hidden = run(_moe_up_kernel, "moe_expert_up", ff // tf,
                 [rows_of(d), w_cols(d, tf), w_cols(d, tf)], own(tf),
                 jax.ShapeDtypeStruct((rows, ff), BF16),
                 [pltpu.VMEM((d, tf), BF16), pltpu.VMEM((d, tf), BF16)], (xb, w_gate, w_up))
    return run(_moe_down_kernel, "moe_expert_down", d // tn,
               [rows_of(ff), w_cols(ff, tn), rows_of(1)], own(tn),
               jax.ShapeDtypeStruct((rows, d), F32),
               [pltpu.VMEM((ff, tn), BF16)], (hidden, w_down, row_w))


def _route(logits, n_groups, n_experts):
    per = n_experts // n_groups
    n = logits.shape[0]
    lg = logits[:, :n_groups]
    le = logits[:, n_groups:n_groups + n_experts]
    gsm = jax.nn.softmax(lg, axis=-1)
    gidx = jnp.argmax(gsm, axis=-1)
    gval = jnp.max(gsm, axis=-1)
    le = jnp.take_along_axis(le.reshape(n, n_groups, per), gidx[:, None, None], axis=1)[:, 0]
    assert TOP_K == 2
    i1 = jnp.argmax(le, axis=-1)
    rest = jnp.where(jnp.arange(per)[None, :] == i1[:, None], -jnp.inf, le)
    i2 = jnp.argmax(rest, axis=-1)
    ev = jnp.stack([jnp.max(le, axis=-1), jnp.max(rest, axis=-1)], axis=-1)
    wts = gval[:, None] * jax.nn.softmax(ev, axis=-1)
    eid = (gidx[:, None] * per + jnp.stack([i1, i2], axis=-1)).astype(jnp.int32)
    return eid, wts


def _dispatch(eid, n_experts, n_rows):
    flat = eid.reshape(-1)
    onehot = (flat[:, None] == jnp.arange(n_experts, dtype=jnp.int32)[None, :]).astype(jnp.int32)
    rank = jnp.take_along_axis(jnp.cumsum(onehot, axis=0) - onehot, flat[:, None], axis=1)[:, 0]
    counts = jnp.sum(onehot, axis=0)
    blocks = (counts + MOE_ROWS - 1) // MOE_ROWS
    bend = jnp.cumsum(blocks)
    bstart = bend - blocks
    slot = bstart[flat] * MOE_ROWS + rank
    n_used = bend[-1:].astype(jnp.int32)
    nblk = n_rows // MOE_ROWS
    block_e = jnp.searchsorted(bend, jnp.arange(nblk, dtype=jnp.int32), side='right')
    block_e = jnp.clip(block_e, 0, n_experts - 1).astype(jnp.int32)
    return slot.astype(jnp.int32), block_e, blocks.astype(jnp.int32), bstart.astype(jnp.int32), n_used


def _moe(x1, n_first, norm_ffn, w_group, b_group, w_router, b_router, w_gate, w_up, w_down):
    n, d = x1.shape
    n_groups = w_group.shape[1]
    n_experts = w_router.shape[1]
    used = n_groups + n_experts
    w_route = jnp.pad(jnp.concatenate([w_group, w_router], axis=1), ((0, 0), (0, LANES - used)))
    b_route = jnp.pad(jnp.concatenate([b_group, b_router]), (0, LANES - used))
    xf, logits = _ffn_norm_router(x1, norm_ffn, w_route, b_route)
    eid, wts = _route(logits, n_groups, n_experts)
    n_assign = n * TOP_K
    n_rows = (n_assign // MOE_ROWS + n_experts) * MOE_ROWS
    slot, block_e, blocks, bstart, n_used = _dispatch(eid, n_experts, n_rows)
    tok = jnp.repeat(jnp.arange(n, dtype=jnp.int32), TOP_K)
    row_tok = jnp.zeros((n_rows,), jnp.int32).at[slot].set(tok)
    row_w = jnp.zeros((n_rows,), F32).at[slot].set(wts.reshape(-1))
    xb = _gather_rows(xf, row_tok, BF16)
    yb = _moe_ffn(xb, row_w.reshape(n_rows, 1), block_e, blocks, bstart, n_used, w_gate, w_up, w_down)
    return _combine(x1, yb, slot.reshape(n, TOP_K), n_first)


def _layer(xp, xs, conv_s, c_s, n_s, m_s, cache_k, cache_v, page_table, lambda_init, p):
    (norm_mix, w_in, b_gate, conv_w, conv_b, q_norm, k_norm, da_lambda, da_subln,
     m_outnorm, w_out, norm_ffn, w_group, b_group, w_router, b_router, w_gate, w_up, w_down) = p
    bp, tp, d = xp.shape
    bs, ts, _ = xs.shape
    n_p, n_s_rows = bp * tp, bs * ts
    heads_m, dk, dv = c_s.shape[1:]
    da_heads = cache_k.shape[2]
    da_w = da_heads * LANES
    m_w = heads_m * dv
    conv_c = conv_w.shape[1]
    assert da_w == m_w == conv_c, "the six projection regions are assumed equally wide"
    region = da_w
    n_reg = 6
    lp = da_lambda.astype(F32)
    lam = jnp.exp(jnp.sum(lp[0] * lp[1])) - jnp.exp(jnp.sum(lp[2] * lp[3])) + lambda_init
    out_scale = 1.0 - lambda_init

    xp2, xs2 = xp.reshape(n_p, d), xs.reshape(n_s_rows, d)
    w_g = jnp.pad(w_in[:, n_reg * region:], ((0, 0), (0, LANES - 2 * heads_m))).astype(BF16)
    proj, gates = _input_projection(xp2, xs2, norm_mix, w_in.astype(BF16), w_g, region, n_reg)
    proj_s = proj[:, n_p:]

    k_f32, q2, k2, v2 = _qk_prep(proj, q_norm, k_norm, tp)
    o_da_p = _prompt_attention(q2, k2, v2, bp, tp, lam, da_subln, out_scale)
    sample3 = lambda a: a[n_p:].reshape(bs, ts, 2 * da_w)
    o_da_s = _paged_attention(sample3(q2), sample3(k2), sample3(v2), cache_k, cache_v, page_table,
                              lam, da_subln, out_scale)

    q_scale = dk ** -0.5
    zeros_state = jnp.zeros((bp, SUBLANES, conv_c), F32)
    mq_p, mk_p = _conv_silu(proj, 3, bp, tp, zeros_state, conv_w, conv_b, q_scale)
    assert ts <= SUBLANES
    qk_s = proj_s[3].reshape(bs, ts, conv_c)
    xs16 = jnp.concatenate([jnp.zeros((bs, 2 * SUBLANES - ts - (CONV_W - 1), conv_c), F32), conv_s, qk_s], axis=1)
    xs16 = xs16.reshape(1, bs * 2 * SUBLANES, conv_c)
    mq_s, mk_s = _conv_silu(xs16, 0, bs, 2 * SUBLANES, jnp.zeros((bs, SUBLANES, conv_c), F32),
                            conv_w, conv_b, q_scale)
    pad8 = lambda a: jnp.pad(a, ((0, 0), (0, SUBLANES - ts), (0, 0)))
    mq_s = pad8(mq_s[:, 2 * SUBLANES - ts:]).astype(F32)
    mk_s = pad8(mk_s[:, 2 * SUBLANES - ts:]).astype(F32)

    def region_spec(r, nb_rows):
        return lambda rows, nc: pl.BlockSpec((1, rows, region), lambda bi, ci: (r, bi * nc + ci, 0))

    zc = jnp.zeros((bp, heads_m, dk, dv), F32)
    o_m_p, c_p, n_pp, m_p = _mlstm(mq_p, mk_p, proj, region_spec(4, n_p), proj, region_spec(5, n_p),
                                   gates[:n_p].reshape(bp, tp, LANES), b_gate, m_outnorm,
                                   zc, jnp.zeros((bp, heads_m, dk), F32), jnp.zeros((bp, heads_m), F32), M_CHUNK)
    seq_spec = lambda rows, nc: pl.BlockSpec((1, rows, region), lambda bi, ci: (bi, ci, 0))
    v_s8 = pad8(proj_s[4].reshape(bs, ts, region))
    og_s8 = pad8(proj_s[5].reshape(bs, ts, region))
    g_s8 = pad8(gates[n_p:].reshape(bs, ts, LANES))
    o_m_s, c_new, n_new, m_new = _mlstm(mq_s, mk_s, v_s8, seq_spec, og_s8, seq_spec, g_s8, b_gate, m_outnorm,
                                        c_s, n_s, m_s, ts)

    o_da = jnp.concatenate([o_da_p, o_da_s.reshape(n_s_rows, da_w).astype(BF16)], axis=0)
    o_m = jnp.concatenate([o_m_p.reshape(n_p, m_w), o_m_s[:, :ts].reshape(n_s_rows, m_w).astype(BF16)], axis=0)
    x1 = _output_projection(o_da, o_m, w_out.astype(BF16), xp2, xs2)
    y_p, y_s = _moe(x1, n_p, norm_ffn, w_group, b_group, w_router, b_router, w_gate, w_up, w_down)

    k4 = lambda a, b_, t_: a.reshape(b_, t_, da_heads, LANES)
    conv_tail = lambda a, b_, t_: a.reshape(b_, t_, conv_c)[:, t_ - (CONV_W - 1):]
    conv_new_s = jnp.concatenate([conv_s, qk_s], axis=1)[:, ts:]
    outs_p = (k4(k_f32[:n_p], bp, tp), k4(proj[2, :n_p], bp, tp), c_p, n_pp, m_p, conv_tail(proj[3, :n_p], bp, tp))
    outs_s = (k4(k_f32[n_p:], bs, ts), k4(proj_s[2], bs, ts), c_new, n_new, m_new, conv_new_s)
    return y_p.reshape(bp, tp, d), y_s.reshape(bs, ts, d), outs_p, outs_s


def kernel(x_prompt, x_sample, cache_k, cache_v, state_C, state_n, state_m, state_conv, page_table, norm_mix, w_in, b_gate, conv_w, conv_b, q_norm, k_norm, da_lambda, da_subln, m_outnorm, w_out, norm_ffn, w_group, b_group, w_router, b_router, w_gate, w_up, w_down):
    depth = w_in.shape[0]
    yp, ys = x_prompt, x_sample
    outs = [[] for _ in range(12)]
    for l in range(depth):
        lambda_init = 0.8 - 0.6 * math.exp(-0.3 * l)
        p = (norm_mix[l], w_in[l], b_gate[l], conv_w[l], conv_b[l], q_norm[l], k_norm[l], da_lambda[l],
             da_subln[l], m_outnorm[l], w_out[l], norm_ffn[l], w_group[l], b_group[l], w_router[l],
             b_router[l], w_gate[l], w_up[l], w_down[l])
        yp, ys, outs_p, outs_s = _layer(yp, ys, state_conv[l], state_C[l], state_n[l], state_m[l],
                                        cache_k[l], cache_v[l], page_table, lambda_init, p)
        for i, a in enumerate(outs_p + outs_s):
            outs[i].append(a)
    return (yp, ys) + tuple(jnp.stack(o) for o in outs)
```

```python
import functools
import math

import numpy as np
import jax
import jax.numpy as jnp
from jax import lax
from jax.experimental import pallas as pl
from jax.experimental.pallas import tpu as pltpu

F32 = jnp.float32
BF16 = jnp.bfloat16

EPS = 1e-6
ALIBI_MAX = 8.0
CONV_W = 4
TOP_K = 2
M_CHUNK = 128
NEG = -1e30

LANES = 128
SUBLANES = 8
VMEM_BYTES_V7X = 64 * 1024 * 1024
VMEM_LIMIT = VMEM_BYTES_V7X - 8 * 1024 * 1024


def _row_tile(n, cap):
    t = cap
    while t > SUBLANES and n % t:
        t //= 2
    assert n % t == 0, (n, t)
    return t


def _params(sem, vmem=None):
    return pltpu.CompilerParams(dimension_semantics=sem, vmem_limit_bytes=vmem or VMEM_LIMIT)


def _two_group_specs(tm, width, nb_first, n_col=1, **spec_kwargs):
    first = pl.BlockSpec((tm, width), lambda i, j: (jnp.minimum(i, nb_first - 1),
                                                    jnp.where(i < nb_first, j, n_col - 1) if n_col > 1 else 0),
                         **spec_kwargs)
    second = pl.BlockSpec((tm, width), lambda i, j: (jnp.maximum(i - nb_first, 0),
                                                     jnp.where(i >= nb_first, j, 0) if n_col > 1 else 0),
                          **spec_kwargs)
    return first, second


def _proj_kernel(xa_ref, xb_ref, g_ref, w_ref, wg_ref, o_ref, og_ref, xn_ref, *, nb_first):
    def normalise(x_ref):
        x = x_ref[...]
        ms = jnp.mean(x * x, axis=-1, keepdims=True)
        xn = ((x * lax.rsqrt(ms + EPS)) * g_ref[...]).astype(BF16)
        xn_ref[...] = xn
        og_ref[...] = jnp.dot(xn, wg_ref[...], preferred_element_type=F32)

    i = pl.program_id(0)
    first_col = pl.program_id(1) == 0
    pl.when(first_col & (i < nb_first))(lambda: normalise(xa_ref))
    pl.when(first_col & (i >= nb_first))(lambda: normalise(xb_ref))
    o_ref[0] = jnp.dot(xn_ref[...], w_ref[...], preferred_element_type=F32)


def _input_projection(xa, xb, g, w, wg, region, n_reg):
    d = xa.shape[1]
    n = xa.shape[0] + xb.shape[0]
    e = n_reg * region
    tm = _row_tile(math.gcd(xa.shape[0], xb.shape[0]), 512)
    nb_first = xa.shape[0] // tm
    tn = min(1024, region)
    per = region // tn
    return pl.pallas_call(
        functools.partial(_proj_kernel, nb_first=nb_first),
        grid=(n // tm, e // tn),
        in_specs=[
            *_two_group_specs(tm, d, nb_first, pipeline_mode=pl.Buffered(1)),
            pl.BlockSpec((1, d), lambda i, j: (0, 0)),
            pl.BlockSpec((d, tn), lambda i, j: (0, j)),
            pl.BlockSpec((d, LANES), lambda i, j: (0, 0)),
        ],
        out_specs=[
            pl.BlockSpec((1, tm, tn), lambda i, j: (j // per, i, j % per)),
            pl.BlockSpec((tm, LANES), lambda i, j: (i, 0)),
        ],
        out_shape=[
            jax.ShapeDtypeStruct((n_reg, n, region), F32),
            jax.ShapeDtypeStruct((n, LANES), F32),
        ],
        scratch_shapes=[pltpu.VMEM((tm, d), BF16)],
        compiler_params=_params(("arbitrary", "arbitrary")),
        name="input_projection",
    )(xa, xb, g.reshape(1, d), w, wg)


SLOPE_TERMS = 4
FEAT = 2 * SLOPE_TERMS
POS_SPLIT = 64
LOG2E = math.log2(math.e)


def _alibi_slopes(n):
    return np.array([2.0 ** (-ALIBI_MAX * (h + 1) / n) for h in range(n)], np.float32)


def _slope_terms(heads):
    import ml_dtypes
    rest = _alibi_slopes(heads).astype(np.float64) * LOG2E
    terms = []
    for _ in range(SLOPE_TERMS):
        t = rest.astype(ml_dtypes.bfloat16).astype(np.float64)
        terms.append(t)
        rest = rest - t
    return np.stack(terms, axis=-1).astype(np.float32)


def _qkprep_kernel(q_ref, k_ref, v_ref, gq_ref, gk_ref, grp_ref, sf_ref, ko_ref, q2_ref, k2_ref, v2_ref,
                   *, heads, half, q_scale, tm, t_seq):
    grp = grp_ref[...]
    lane = lax.broadcasted_iota(jnp.int32, (1, LANES), 1)
    lo_half = lane < half
    fl = lane & (half - 1)
    pos = (pl.program_id(0) * tm) % t_seq + lax.broadcasted_iota(jnp.int32, (tm, 1), 0)
    pos_hi = (pos - (pos & (POS_SPLIT - 1))).astype(F32)
    pos_lo = (pos & (POS_SPLIT - 1)).astype(F32)
    kfeat = jnp.where(fl < FEAT, jnp.where((fl & 1) == 0, pos_hi, pos_lo), 0.0)
    ones_col = jnp.broadcast_to(jnp.where(lane == 0, 1.0, 0.0), (tm, LANES)).astype(BF16)

    def half_norm(x, g):
        sq = x * x
        hi = sq.astype(BF16)
        lo = (sq - hi.astype(F32)).astype(BF16)
        s = jnp.dot(hi, grp, preferred_element_type=F32) + jnp.dot(lo, grp, preferred_element_type=F32)
        return (x * lax.rsqrt(s * (1.0 / half) + EPS)) * g

    for h in range(heads):
        sl = slice(h * LANES, (h + 1) * LANES)
        m0 = slice(2 * h * LANES, (2 * h + 1) * LANES)
        m1 = slice((2 * h + 1) * LANES, (2 * h + 2) * LANES)
        qn = half_norm(q_ref[0, :, sl], gq_ref[...]) * q_scale
        sf = sf_ref[h:h + 1, :]
        q2_ref[:, m0] = jnp.where(lo_half, qn, sf).astype(BF16)
        q2_ref[:, m1] = jnp.where(lo_half, sf, qn).astype(BF16)
        kn = half_norm(k_ref[0, :, sl], gk_ref[...])
        ko_ref[:, sl] = kn
        k2_ref[:, m0] = jnp.where(lo_half, kn, kfeat).astype(BF16)
        k2_ref[:, m1] = jnp.where(lo_half, kfeat, kn).astype(BF16)
        v2_ref[:, m0] = v_ref[0, :, sl].astype(BF16)
        v2_ref[:, m1] = ones_col


def _qk_prep(proj, q_norm, k_norm, t_seq):
    _, n, w = proj.shape
    heads = w // LANES
    half = q_norm.shape[0]
    assert 2 * half == LANES and FEAT <= half
    tm = _row_tile(math.gcd(n, t_seq), 256)
    assert t_seq <= POS_SPLIT * 256, "key positions must split into two bf16-exact factors"
    lane = np.arange(LANES)
    grp = jnp.asarray((lane[:, None] // half == lane[None, :] // half).astype(np.float32), BF16)
    gq = jnp.concatenate([q_norm, q_norm]).reshape(1, LANES)
    gk = jnp.concatenate([k_norm, k_norm]).reshape(1, LANES)
    terms = _slope_terms(heads)
    fl = lane % half
    sf = jnp.asarray(np.where(fl[None, :] < FEAT, terms[:, np.minimum(fl // 2, SLOPE_TERMS - 1)], 0.0))
    row = lambda r: pl.BlockSpec((1, tm, w), lambda i: (r, i, 0))
    full = lambda shape: pl.BlockSpec(shape, lambda i: (0, 0))
    out1 = pl.BlockSpec((tm, w), lambda i: (i, 0))
    out2 = pl.BlockSpec((tm, 2 * w), lambda i: (i, 0))
    return pl.pallas_call(
        functools.partial(_qkprep_kernel, heads=heads, half=half, q_scale=half ** -0.5 * LOG2E,
                          tm=tm, t_seq=t_seq),
        grid=(n // tm,),
        in_specs=[row(0), row(1), row(2), full((1, LANES)), full((1, LANES)), full((LANES, LANES)),
                  full((heads, LANES))],
        out_specs=[out1, out2, out2, out2],
        out_shape=[
            jax.ShapeDtypeStruct((n, w), F32),
            jax.ShapeDtypeStruct((n, 2 * w), BF16),
            jax.ShapeDtypeStruct((n, 2 * w), BF16),
            jax.ShapeDtypeStruct((n, 2 * w), BF16),
        ],
        compiler_params=_params(("arbitrary",)),
        name="qk_prep",
    )(proj, proj, proj, gq, gk, grp, sf)


def _subln(o, g, out_scale):
    ms = jnp.mean(o * o, axis=-1, keepdims=True)
    return ((o * lax.rsqrt(ms + EPS)) * g) * out_scale


def _attn_kernel(lam_ref, q_ref, k_ref, v_ref, gs_ref, o_ref, m0_ref, a0_ref, m1_ref, a1_ref,
                 *, tq, out_scale):
    i = pl.program_id(2)
    q = q_ref[...]
    qs = (q[:, :LANES], q[:, LANES:])
    states = ((m0_ref, a0_ref), (m1_ref, a1_ref))
    for m_ref, a_ref in states:
        m_ref[...] = jnp.full(m_ref.shape, NEG, F32)
        a_ref[...] = jnp.zeros(a_ref.shape, F32)
    causal = lax.broadcasted_iota(jnp.int32, (tq, tq), 0) >= lax.broadcasted_iota(jnp.int32, (tq, tq), 1)

    def scores(j):
        kt = k_ref[pl.ds(pl.multiple_of(j * tq, tq), tq), :]
        return tuple(lax.dot_general(qs[c], kt[:, c * LANES:(c + 1) * LANES], (((1,), (1,)), ((), ())),
                                     preferred_element_type=F32) for c in range(2))

    def accumulate(j, s_maps, masked):
        vt = v_ref[pl.ds(pl.multiple_of(j * tq, tq), tq), :]
        for s, (m_ref, a_ref) in zip(s_maps, states):
            if masked:
                s = jnp.where(causal, s, NEG)
            m_old = m_ref[...]
            m_new = jnp.maximum(m_old, jnp.max(s, axis=-1, keepdims=True))
            p = jnp.exp2(s - m_new)
            a_ref[...] = jnp.exp2(m_old - m_new) * a_ref[...] + jnp.dot(p.astype(BF16), vt,
                                                                         preferred_element_type=F32)
            m_ref[...] = m_new

    def body(j, s_maps):
        s_next = scores(j + 1)
        accumulate(j, s_maps, False)
        return s_next

    accumulate(i, lax.fori_loop(0, i, body, scores(0)), True)
    a0 = a0_ref[...]
    a1 = a1_ref[...]
    o = a0[:, :LANES] / a0[:, LANES:LANES + 1] - lam_ref[0] * (a1[:, :LANES] / a1[:, LANES:LANES + 1])
    o_ref[...] = _subln(o, gs_ref[...], out_scale).astype(o_ref.dtype)


def _prompt_attention(q2, k2, v2, n_batch, t, lam, subln, out_scale):
    heads = q2.shape[1] // (2 * LANES)
    tq = _row_tile(t, 512)
    nq = t // tq
    kv = pl.BlockSpec((t, 2 * LANES), lambda bi, hi, i, lm: (bi, hi))
    return pl.pallas_call(
        functools.partial(_attn_kernel, tq=tq, out_scale=out_scale),
        grid_spec=pltpu.PrefetchScalarGridSpec(
            num_scalar_prefetch=1,
            grid=(n_batch, heads, nq),
            in_specs=[pl.BlockSpec((tq, 2 * LANES), lambda bi, hi, i, lm: (bi * nq + i, hi)), kv, kv,
                      pl.BlockSpec((1, LANES), lambda bi, hi, i, lm: (0, 0))],
            out_specs=pl.BlockSpec((tq, LANES), lambda bi, hi, i, lm: (bi * nq + i, hi)),
            scratch_shapes=[pltpu.VMEM((tq, 1), F32), pltpu.VMEM((tq, 2 * LANES), F32)] * 2,
        ),
        out_shape=jax.ShapeDtypeStruct((n_batch * t, heads * LANES), BF16),
        compiler_params=_params(("arbitrary", "arbitrary", "arbitrary")),
        name="prompt_diff_attention",
    )(lam.reshape(1), q2, k2, v2, subln.reshape(1, LANES))


QSLOTS = 4
NEW_SLOTS = 8


def _paged_attn_kernel(pt_ref, lam_ref, *refs, pages, page, out_scale):
    kp_refs = refs[:pages]
    vp_refs = refs[pages:2 * pages]
    (q_ref, kn_ref, vn_ref, bias_ref, biasn_ref, slope_ref, gs_ref, o_ref, m_ref, l_ref, a_ref) = refs[2 * pages:]
    g = pl.program_id(1)
    q = q_ref[0]
    slope_c = slope_ref[...]
    nt = (((1,), (1,)), ((), ()))

    def update(tiles, first):
        m_old = jnp.full(m_ref.shape, NEG, F32) if first else m_ref[...]
        m_new = m_old
        for s, shift_c, _ in tiles:
            m_new = jnp.maximum(m_new, jnp.max(s, axis=-1, keepdims=True) + shift_c)
        psum = None
        pv = None
        for s, shift_c, v_tile in tiles:
            p = jnp.exp2(s - (m_new - shift_c))
            ps = jnp.sum(p, axis=-1, keepdims=True)
            d = jnp.dot(p.astype(BF16), v_tile, preferred_element_type=F32)
            psum = ps if psum is None else psum + ps
            pv = d if pv is None else pv + d
        if first:
            l_ref[...] = psum
            a_ref[...] = pv
        else:
            alpha = jnp.exp2(m_old - m_new)
            l_ref[...] = alpha * l_ref[...] + psum
            a_ref[...] = alpha * a_ref[...] + pv
        m_ref[...] = m_new

    @pl.when(g == 0)
    def _():
        s = lax.dot_general(q, kn_ref[0], nt, preferred_element_type=F32) + biasn_ref[...]
        update([(s, jnp.zeros_like(slope_c), vn_ref[0])], True)

    tiles = []
    for p in range(pages):
        kt = kp_refs[p][0].astype(BF16)
        s = lax.dot_general(q, kt, nt, preferred_element_type=F32) + bias_ref[...]
        first_pos = ((g * pages + p) * page).astype(F32)
        tiles.append((s, slope_c * first_pos, vp_refs[p][0].astype(BF16)))
    update(tiles, False)

    @pl.when(g == pl.num_programs(1) - 1)
    def _():
        a = a_ref[...] / l_ref[...]
        half_rows = a.shape[0] // 2
        o = a[:half_rows] - lam_ref[0] * a[half_rows:]
        o_ref[0] = _subln(o, gs_ref[...], out_scale)


def _paged_attention(q2, k2, v2, cache_k, cache_v, page_table, lam, subln, out_scale):
    b, tq, _ = q2.shape
    n_pool, page, heads, _ = cache_k.shape
    half = LANES // 2
    n_pages = page_table.shape[1]
    past_len = n_pages * page
    pages = math.gcd(n_pages, 8)
    rows = 2 * heads * QSLOTS
    assert tq <= QSLOTS and tq <= NEW_SLOTS
    q5 = q2.reshape(b, tq, heads, 2, LANES)
    zero = jnp.zeros((b, tq, heads, half), BF16)
    qmaps = jnp.stack([jnp.concatenate([q5[:, :, :, 0, :half], zero], axis=-1),
                       jnp.concatenate([zero, q5[:, :, :, 1, half:]], axis=-1)], axis=1)
    qrows = jnp.pad(qmaps.transpose(0, 1, 3, 2, 4), ((0, 0), (0, 0), (0, 0), (0, QSLOTS - tq), (0, 0)))
    qrows = qrows.reshape(b, rows, LANES)
    k5 = k2.reshape(b, tq, heads, 2, LANES)
    kn = jnp.concatenate([k5[:, :, :, 0, :half], k5[:, :, :, 1, half:]], axis=-1)
    vn = v2.reshape(b, tq, heads, 2, LANES)[:, :, :, 0, :]
    pad_t = ((0, 0), (0, NEW_SLOTS - tq), (0, 0), (0, 0))
    kn = jnp.pad(kn, pad_t).reshape(b, NEW_SLOTS * heads, LANES)
    vn = jnp.pad(vn, pad_t).reshape(b, NEW_SLOTS * heads, LANES)
    slope2 = _slope_terms(heads).astype(np.float64).sum(axis=-1)
    r = np.arange(rows)
    row_head = (r // QSLOTS) % heads
    row_q = r % QSLOTS

    def bias_matrix(n_tok, key_pos0):
        col = np.arange(n_tok * heads)
        col_tok, col_head = col // heads, col % heads
        dist = (past_len + row_q)[:, None] - (key_pos0 + col_tok)[None, :]
        ok = (row_head[:, None] == col_head[None, :]) & (dist >= 0)
        return jnp.asarray(np.where(ok, -slope2[row_head][:, None] * dist, NEG).astype(np.float32))

    bias_page = bias_matrix(page, 0)
    new_tok = np.arange(NEW_SLOTS * heads) // heads
    bias_new = jnp.where(jnp.asarray(new_tok < tq)[None, :], bias_matrix(NEW_SLOTS, past_len), NEG)
    slope_col = jnp.asarray(slope2[row_head].astype(np.float32).reshape(rows, 1))
    ck = cache_k.reshape(n_pool, page * heads, LANES)
    cv = cache_v.reshape(n_pool, page * heads, LANES)

    def page_spec(p):
        return pl.BlockSpec((1, page * heads, LANES),
                            lambda bi, g, pt, lm: (pt[bi * n_pages + g * pages + p], 0, 0))

    per_b = lambda shape: pl.BlockSpec(shape, lambda bi, g, pt, lm: (bi,) + (0,) * (len(shape) - 1))
    const = lambda shape: pl.BlockSpec(shape, lambda bi, g, pt, lm: (0,) * len(shape))
    out = pl.pallas_call(
        functools.partial(_paged_attn_kernel, pages=pages, page=page, out_scale=out_scale),
        grid_spec=pltpu.PrefetchScalarGridSpec(
            num_scalar_prefetch=2,
            grid=(b, n_pages // pages),
            in_specs=[page_spec(p) for p in range(pages)] * 2 + [
                per_b((1, rows, LANES)), per_b((1, NEW_SLOTS * heads, LANES)), per_b((1, NEW_SLOTS * heads, LANES)),
                const((rows, page * heads)), const((rows, NEW_SLOTS * heads)), const((rows, 1)),
                const((1, LANES))],
            out_specs=per_b((1, rows // 2, LANES)),
            scratch_shapes=[pltpu.VMEM((rows, 1), F32), pltpu.VMEM((rows, 1), F32),
                            pltpu.VMEM((rows, LANES), F32)],
        ),
        out_shape=jax.ShapeDtypeStruct((b, rows // 2, LANES), F32),
        compiler_params=_params(("arbitrary", "arbitrary")),
        name="paged_diff_attention",
    )(page_table.reshape(-1), lam.reshape(1), *([ck] * pages), *([cv] * pages),
      qrows, kn, vn, bias_page, bias_new, slope_col, subln.reshape(1, LANES))
    return out.reshape(b, heads, QSLOTS, LANES)[:, :, :tq].transpose(0, 2, 1, 3).reshape(b, tq, heads * LANES)


def _conv_kernel(x_ref, st_ref, w_ref, b_ref, qo_ref, ko_ref, buf_ref, *, tt, q_scale, carry):
    @pl.when(pl.program_id(1) == 0)
    def _():
        buf_ref[0:SUBLANES, :] = st_ref[0]

    x = x_ref[0]
    buf_ref[SUBLANES:SUBLANES + tt, :] = x
    acc = b_ref[...] + w_ref[CONV_W - 1:CONV_W, :] * x
    for k in range(1, CONV_W):
        acc = acc + w_ref[CONV_W - 1 - k:CONV_W - k, :] * buf_ref[SUBLANES - k:SUBLANES - k + tt, :]
    u = acc * jax.nn.sigmoid(acc)
    c = u.shape[1] // 2
    qo_ref[0] = (u[:, :c] * q_scale).astype(qo_ref.dtype)
    ko_ref[0] = u[:, c:].astype(ko_ref.dtype)
    if carry:
        buf_ref[0:SUBLANES, :] = buf_ref[tt:tt + SUBLANES, :]


def _conv_silu(x, region, n_batch, t, state8, conv_w, conv_b, q_scale):
    c = x.shape[2]
    tt = _row_tile(t, 256)
    nt = t // tt
    w8 = jnp.pad(conv_w, ((0, SUBLANES - CONV_W), (0, 0)))
    out = pl.BlockSpec((1, tt, c // 2), lambda bi, i: (bi, i, 0))
    return pl.pallas_call(
        functools.partial(_conv_kernel, tt=tt, q_scale=q_scale, carry=nt > 1),
        grid=(n_batch, nt),
        in_specs=[
            pl.BlockSpec((1, tt, c), lambda bi, i: (region, bi * nt + i, 0)),
            pl.BlockSpec((1, SUBLANES, c), lambda bi, i: (bi, 0, 0)),
            pl.BlockSpec((SUBLANES, c), lambda bi, i: (0, 0)),
            pl.BlockSpec((1, c), lambda bi, i: (0, 0)),
        ],
        out_specs=[out, out],
        out_shape=[jax.ShapeDtypeStruct((n_batch, t, c // 2), BF16)] * 2,
        scratch_shapes=[pltpu.VMEM((tt + SUBLANES, c), F32)],
        compiler_params=_params(("arbitrary", "arbitrary")),
        name="mlstm_conv_silu",
    )(x, state8, w8, conv_b.reshape(1, c))


def _mlstm_kernel(q_ref, k_ref, v_ref, og_ref, g_ref, bg_ref, on_ref, c0_ref, n0_ref, m0_ref,
                  o_ref, c_ref, n_ref, m_ref, cs_ref, ns_ref, ms_ref, *, heads, dk, dv, rows, chunk, t_valid):
    L = chunk
    ci = pl.program_id(1)

    @pl.when(ci == 0)
    def _():
        cs_ref[...] = c0_ref[0]
        ns_ref[...] = n0_ref[0]
        ms_ref[...] = m0_ref[0]

    def pad_rows(a):
        if rows == L:
            return a
        return jnp.concatenate([a, jnp.zeros((L - rows, a.shape[1]), a.dtype)], axis=0)

    gates = pad_rows(g_ref[0] + bg_ref[...])
    lf_all = jnp.minimum(gates, 0.0) - jnp.log1p(jnp.exp(-jnp.abs(gates)))
    q_all = pad_rows(q_ref[0]).astype(BF16)
    k_all = pad_rows(k_ref[0]).astype(BF16)
    v_all = pad_rows(v_ref[0]).astype(BF16)
    r_i = lax.broadcasted_iota(jnp.int32, (L, L), 0)
    c_i = lax.broadcasted_iota(jnp.int32, (L, L), 1)
    eye = r_i == c_i
    tril = c_i <= r_i
    valid = lax.broadcasted_iota(jnp.int32, (L, 1), 0) < t_valid

    def to_row(col):
        return jnp.sum(jnp.where(eye, col, 0.0), axis=0, keepdims=True)

    ms_all = ms_ref[...]
    ns_all = ns_ref[...]
    c_olds = [cs_ref[h] for h in range(heads)]
    outs, c_news, n_news, m_news = [], [], [], []
    for h in range(heads):
        ig_c = jnp.where(valid, gates[:, h:h + 1], NEG)
        lf_c = jnp.where(valid, lf_all[:, heads + h:heads + h + 1], 0.0)
        ig_r = to_row(ig_c)
        lf_r = to_row(lf_c)
        b_c = jnp.sum(jnp.where(tril, lf_r, 0.0), axis=1, keepdims=True)
        b_r = to_row(b_c)
        m_prev = ms_all[h:h + 1, 0:1]
        dmat = jnp.where(tril, (b_c - b_r) + ig_r, NEG)
        inter = b_c + m_prev
        mt = jnp.maximum(inter, jnp.max(dmat, axis=1, keepdims=True))
        qh = q_all[:, h * dk:(h + 1) * dk]
        kh = k_all[:, h * dk:(h + 1) * dk]
        vh = v_all[:, h * dv:(h + 1) * dv]
        c_old = c_olds[h]
        n_old = ns_all[h:h + 1, :]
        s = lax.dot_general(qh, kh, (((1,), (1,)), ((), ())), preferred_element_type=F32)
        wm = s * jnp.exp(dmat - mt)
        e_in = jnp.exp(inter - mt)
        num = e_in * jnp.dot(qh, c_old.astype(BF16), preferred_element_type=F32) \
            + jnp.dot(wm.astype(BF16), vh, preferred_element_type=F32)
        den = e_in * jnp.sum(qh.astype(F32) * n_old, axis=1, keepdims=True) + jnp.sum(wm, axis=1, keepdims=True)
        hid = num / jnp.maximum(jnp.abs(den), jnp.exp(-mt))
        ms_h = jnp.mean(hid * hid, axis=-1, keepdims=True)
        hn = (hid * lax.rsqrt(ms_h + EPS)) * on_ref[...]
        gate = jax.nn.sigmoid(og_ref[0, :, h * dv:(h + 1) * dv])
        outs.append((gate * hn[:rows]).astype(o_ref.dtype))
        b_last = jnp.sum(lf_r, axis=1, keepdims=True)
        g_c = (b_last - b_c) + ig_c
        g_r = (b_last - b_r) + ig_r
        m_new = jnp.maximum(b_last + m_prev, jnp.max(g_r, axis=1, keepdims=True))
        e_old = jnp.exp((b_last + m_prev) - m_new)
        kw = kh.astype(F32) * jnp.exp(g_c - m_new)
        c_news.append(e_old * c_old + lax.dot_general(kw.astype(BF16), vh, (((0,), (0,)), ((), ())),
                                                      preferred_element_type=F32))
        n_news.append(e_old * n_old + jnp.sum(kw, axis=0, keepdims=True))
        m_news.append(jnp.broadcast_to(m_new, (1, LANES)))

    for h in range(heads):
        o_ref[0, :, h * dv:(h + 1) * dv] = outs[h]
        cs_ref[h] = c_news[h]
    ns_ref[...] = jnp.concatenate(n_news, axis=0)
    ms_ref[...] = jnp.concatenate(m_news, axis=0)

    @pl.when(ci == pl.num_programs(1) - 1)
    def _():
        c_ref[0] = cs_ref[...]
        n_ref[0] = ns_ref[...]
        m_ref[0] = ms_ref[...]


def _mlstm(q, k, v_src, v_spec, og_src, og_spec, gates, b_gate, outnorm, c0, n0, m0, t_valid):
    b, t, _ = q.shape
    heads, dk, dv = c0.shape[1:]
    rows = min(t, M_CHUNK)
    nc = t // rows
    bg = jnp.pad(b_gate, (0, LANES - b_gate.shape[0])).reshape(1, LANES)
    m0b = jnp.broadcast_to(m0[:, :, None], (b, heads, LANES))
    blk = lambda w: pl.BlockSpec((1, rows, w), lambda bi, ci: (bi, ci, 0))
    per_b = lambda shape: pl.BlockSpec(shape, lambda bi, ci: (bi,) + (0,) * (len(shape) - 1))
    const = lambda shape: pl.BlockSpec(shape, lambda bi, ci: (0,) * len(shape))
    chunk = M_CHUNK if rows == M_CHUNK else 2 * SUBLANES
    kernel = functools.partial(_mlstm_kernel, heads=heads, dk=dk, dv=dv, rows=rows, chunk=chunk, t_valid=t_valid)
    o, c, n, m = pl.pallas_call(
        kernel,
        grid=(b, nc),
        in_specs=[blk(heads * dk), blk(heads * dk), v_spec(rows, nc), og_spec(rows, nc), blk(LANES),
                  const((1, LANES)), const((1, dv)), per_b((1, heads, dk, dv)), per_b((1, heads, dk)),
                  per_b((1, heads, LANES))],
        out_specs=[blk(heads * dv), per_b((1, heads, dk, dv)), per_b((1, heads, dk)), per_b((1, heads, LANES))],
        out_shape=[jax.ShapeDtypeStruct((b, t, heads * dv), BF16 if rows == M_CHUNK else F32),
                   jax.ShapeDtypeStruct((b, heads, dk, dv), F32),
                   jax.ShapeDtypeStruct((b, heads, dk), F32),
                   jax.ShapeDtypeStruct((b, heads, LANES), F32)],
        scratch_shapes=[pltpu.VMEM((heads, dk, dv), F32), pltpu.VMEM((heads, dk), F32),
                        pltpu.VMEM((heads, LANES), F32)],
        compiler_params=_params(("arbitrary", "arbitrary")),
        name="mlstm_chunkwise",
    )(q, k, v_src, og_src, gates, bg, outnorm.reshape(1, dv), c0, n0, m0b)
    return o, c, n, m[:, :, 0]


def _outproj_kernel(a_ref, b_ref, wa_ref, wb_ref, xa_ref, xb_ref, o_ref, *, nb_first):
    mixed = jnp.dot(a_ref[...], wa_ref[...], preferred_element_type=F32) \
        + jnp.dot(b_ref[...], wb_ref[...], preferred_element_type=F32)
    i = pl.program_id(0)

    @pl.when(i < nb_first)
    def _():
        o_ref[...] = xa_ref[...] + mixed

    @pl.when(i >= nb_first)
    def _():
        o_ref[...] = xb_ref[...] + mixed


def _output_projection(a, bmix, w, xa, xb):
    n, wa = a.shape
    wb = bmix.shape[1]
    d = w.shape[1]
    tm = _row_tile(math.gcd(xa.shape[0], xb.shape[0]), 512)
    nb_first = xa.shape[0] // tm
    tn = min(1024, d)
    return pl.pallas_call(
        functools.partial(_outproj_kernel, nb_first=nb_first),
        grid=(n // tm, d // tn),
        in_specs=[
            pl.BlockSpec((tm, wa), lambda i, j: (i, 0)),
            pl.BlockSpec((tm, wb), lambda i, j: (i, 0)),
            pl.BlockSpec((wa, tn), lambda i, j: (0, j)),
            pl.BlockSpec((wb, tn), lambda i, j: (wa // wb, j)),
            *_two_group_specs(tm, tn, nb_first, n_col=d // tn),
        ],
        out_specs=pl.BlockSpec((tm, tn), lambda i, j: (i, j)),
        out_shape=jax.ShapeDtypeStruct((n, d), F32),
        compiler_params=_params(("arbitrary", "arbitrary")),
        name="output_projection",
    )(a, bmix, w, w, xa, xb)


def _router_kernel(x_ref, g_ref, whi_ref, wlo_ref, b_ref, xf_ref, lg_ref):
    x = x_ref[...]
    ms = jnp.mean(x * x, axis=-1, keepdims=True)
    xn = (x * lax.rsqrt(ms + EPS)) * g_ref[...]
    hi = xn.astype(BF16)
    lo = (xn - hi.astype(F32)).astype(BF16)
    xf_ref[...] = xn
    lg_ref[...] = b_ref[...] + jnp.dot(hi, whi_ref[...], preferred_element_type=F32) \
        + (jnp.dot(hi, wlo_ref[...], preferred_element_type=F32)
           + jnp.dot(lo, whi_ref[...], preferred_element_type=F32))


def _ffn_norm_router(x, g, w_route, b_route):
    n, d = x.shape
    tm = _row_tile(n, 256)
    whi = w_route.astype(BF16)
    wlo = (w_route - whi.astype(F32)).astype(BF16)
    const = lambda shape: pl.BlockSpec(shape, lambda i: (0, 0))
    return pl.pallas_call(
        _router_kernel,
        grid=(n // tm,),
        in_specs=[pl.BlockSpec((tm, d), lambda i: (i, 0)), const((1, d)), const((d, LANES)),
                  const((d, LANES)), const((1, LANES))],
        out_specs=[pl.BlockSpec((tm, d), lambda i: (i, 0)), pl.BlockSpec((tm, LANES), lambda i: (i, 0))],
        out_shape=[jax.ShapeDtypeStruct((n, d), F32), jax.ShapeDtypeStruct((n, LANES), F32)],
        compiler_params=_params(("arbitrary",)),
        name="ffn_norm_router",
    )(x, g.reshape(1, d), whi, wlo, b_route.reshape(1, LANES))


MOE_ROWS = 256
MOE_FF_TILE = 512
MOE_OUT_TILE = 2048


def _row_copy(src_ref, src_row, dst_ref, dst_row, sem):
    return pltpu.make_async_copy(src_ref.at[pl.ds(src_row, 1)], dst_ref.at[pl.ds(dst_row, 1)], sem)


DMA_UNROLL = 8


def _gather_rows_kernel(nu_ref, ids_ref, src_ref, o_ref, buf_ref, sem, *, rows):
    def start(r, carry):
        _row_copy(src_ref, ids_ref[0, 0, r], buf_ref, r, sem).start()
        return carry

    def wait(r, carry):
        _row_copy(src_ref, 0, buf_ref, r, sem).wait()
        return carry

    @pl.when(pl.program_id(0) < nu_ref[0])
    def _():
        lax.fori_loop(0, rows, start, 0, unroll=DMA_UNROLL)
        lax.fori_loop(0, rows, wait, 0, unroll=DMA_UNROLL)
        o_ref[...] = buf_ref[...].astype(o_ref.dtype)

    @pl.when(pl.program_id(0) >= nu_ref[0])
    def _():
        o_ref[...] = jnp.zeros(o_ref.shape, o_ref.dtype)


def _gather_rows(src, ids, n_used, out_dtype):
    n_rows = ids.shape[0]
    d = src.shape[1]
    nblk = n_rows // MOE_ROWS
    return pl.pallas_call(
        functools.partial(_gather_rows_kernel, rows=MOE_ROWS),
        grid_spec=pltpu.PrefetchScalarGridSpec(
            num_scalar_prefetch=1,
            grid=(nblk,),
            in_specs=[pl.BlockSpec((1, 1, MOE_ROWS), lambda i, nu: (i, 0, 0), memory_space=pltpu.SMEM),
                      pl.BlockSpec(memory_space=pl.ANY)],
            out_specs=pl.BlockSpec((MOE_ROWS, d), lambda i, nu: (i, 0)),
            scratch_shapes=[pltpu.VMEM((MOE_ROWS, d), src.dtype), pltpu.SemaphoreType.DMA(())],
        ),
        out_shape=jax.ShapeDtypeStruct((n_rows, d), out_dtype),
        compiler_params=_params(("arbitrary",)),
        name="moe_gather_rows",
    )(n_used, ids.reshape(nblk, 1, MOE_ROWS), src)


COMBINE_ROWS = 128


def _combine_kernel(slots_ref, x_ref, yb_ref, oa_ref, ob_ref, buf_ref, sem, *, tc, nb_first):
    n_copies = TOP_K * tc

    def start(r, carry):
        _row_copy(yb_ref, slots_ref[0, 0, r], buf_ref, r, sem).start()
        return carry

    def wait(r, carry):
        _row_copy(yb_ref, 0, buf_ref, r, sem).wait()
        return carry

    lax.fori_loop(0, n_copies, start, 0, unroll=DMA_UNROLL)
    lax.fori_loop(0, n_copies, wait, 0, unroll=DMA_UNROLL)
    acc = buf_ref[0:tc, :]
    for k in range(1, TOP_K):
        acc = acc + buf_ref[k * tc:(k + 1) * tc, :]
    i = pl.program_id(0)

    @pl.when(i < nb_first)
    def _():
        oa_ref[...] = x_ref[...] + acc

    @pl.when(i >= nb_first)
    def _():
        ob_ref[...] = x_ref[...] + acc


def _combine(x, yb, slot2, n_first):
    n, d = x.shape
    tc = _row_tile(math.gcd(n_first, n - n_first), COMBINE_ROWS)
    nblk = n // tc
    nb_first = n_first // tc
    slots = slot2.reshape(nblk, tc, TOP_K).transpose(0, 2, 1).reshape(nblk, 1, TOP_K * tc)
    return pl.pallas_call(
        functools.partial(_combine_kernel, tc=tc, nb_first=nb_first),
        grid=(nblk,),
        in_specs=[pl.BlockSpec((1, 1, TOP_K * tc), lambda i: (i, 0, 0), memory_space=pltpu.SMEM),
                  pl.BlockSpec((tc, d), lambda i: (i, 0)),
                  pl.BlockSpec(memory_space=pl.ANY)],
        out_specs=[pl.BlockSpec((tc, d), lambda i: (jnp.minimum(i, nb_first - 1), 0)),
                   pl.BlockSpec((tc, d), lambda i: (jnp.maximum(i - nb_first, 0), 0))],
        out_shape=[jax.ShapeDtypeStruct((n_first, d), F32), jax.ShapeDtypeStruct((n - n_first, d), F32)],
        scratch_shapes=[pltpu.VMEM((TOP_K * tc, d), F32), pltpu.SemaphoreType.DMA(())],
        compiler_params=_params(("arbitrary",)),
        name="moe_combine",
    )(slots, x, yb)


def _weights_changed(t, ib_ref, is_ref, be_ref):
    prev = jnp.maximum(t - 1, 0)
    return (t == 0) | (be_ref[ib_ref[t]] != be_ref[ib_ref[prev]]) | (is_ref[t] != is_ref[prev])


def _moe_up_kernel(ib_ref, is_ref, be_ref, ni_ref, x_ref, wg_ref, wu_ref, h_ref, wgb_ref, wub_ref):
    t = pl.program_id(0)

    @pl.when(t >= ni_ref[0])
    def _():
        h_ref[...] = jnp.zeros(h_ref.shape, h_ref.dtype)

    @pl.when(t < ni_ref[0])
    def _():
        @pl.when(_weights_changed(t, ib_ref, is_ref, be_ref))
        def _():
            wgb_ref[...] = wg_ref[0].astype(BF16)
            wub_ref[...] = wu_ref[0].astype(BF16)

        x = x_ref[...]
        gt = jnp.dot(x, wgb_ref[...], preferred_element_type=F32)
        up = jnp.dot(x, wub_ref[...], preferred_element_type=F32)
        h_ref[...] = ((gt * jax.nn.sigmoid(gt)) * up).astype(h_ref.dtype)


def _moe_down_kernel(ib_ref, is_ref, be_ref, ni_ref, h_ref, wd_ref, rw_ref, o_ref, wdb_ref):
    t = pl.program_id(0)

    @pl.when(t >= ni_ref[0])
    def _():
        o_ref[...] = jnp.zeros(o_ref.shape, F32)

    @pl.when(t < ni_ref[0])
    def _():
        @pl.when(_weights_changed(t, ib_ref, is_ref, be_ref))
        def _():
            wdb_ref[...] = wd_ref[0].astype(BF16)

        o_ref[...] = jnp.dot(h_ref[...], wdb_ref[...], preferred_element_type=F32) * rw_ref[...]


def _item_tables(block_e, blocks, bstart, n_used, nblk, nsplit):
    b = jnp.arange(nblk, dtype=jnp.int32)
    s = jnp.arange(nsplit, dtype=jnp.int32)
    e = block_e
    pos_live = nsplit * bstart[e][:, None] + s[None, :] * blocks[e][:, None] + (b - bstart[e])[:, None]
    pos_idle = b[:, None] * nsplit + s[None, :]
    pos = jnp.where((b < n_used[0])[:, None], pos_live, pos_idle).reshape(-1)
    item_b = jnp.zeros((nblk * nsplit,), jnp.int32).at[pos].set(jnp.repeat(b, nsplit))
    item_s = jnp.zeros((nblk * nsplit,), jnp.int32).at[pos].set(jnp.tile(s, nblk))
    return item_b, item_s, (n_used * nsplit).astype(jnp.int32)


def _moe_ffn(xb, row_w, block_e, blocks, bstart, n_used, w_gate, w_up, w_down):
    rows, d = xb.shape
    ff = w_gate.shape[2]
    tf = min(MOE_FF_TILE, ff)
    tn = min(MOE_OUT_TILE, d)
    nblk = rows // MOE_ROWS

    def live(t, ni):
        return jnp.minimum(t, ni[0] - 1)

    def run(kernel, name, nsplit, in_specs, out_spec, out_shape, scratch, args):
        item_b, item_s, n_items = _item_tables(block_e, blocks, bstart, n_used, nblk, nsplit)
        return pl.pallas_call(
            kernel,
            grid_spec=pltpu.PrefetchScalarGridSpec(
                num_scalar_prefetch=4, grid=(nblk * nsplit,), in_specs=in_specs, out_specs=out_spec,
                scratch_shapes=scratch),
            out_shape=out_shape,
            compiler_params=_params(("arbitrary",)),
            name=name,
        )(item_b, item_s, block_e, n_items, *args)

    rows_of = lambda w: pl.BlockSpec((MOE_ROWS, w), lambda t, ib, s, be, ni: (ib[live(t, ni)], 0))
    w_cols = lambda k, w: pl.BlockSpec(
        (1, k, w), lambda t, ib, s, be, ni: (be[ib[live(t, ni)]], 0, s[live(t, ni)]))
    own = lambda w: pl.BlockSpec((MOE_ROWS, w), lambda t, ib, s, be, ni: (ib[t], s[t]))
    hidden = run(_moe_up_kernel, "moe_expert_up", ff // tf,
                 [rows_of(d), w_cols(d, tf), w_cols(d, tf)], own(tf),
                 jax.ShapeDtypeStruct((rows, ff), BF16),
                 [pltpu.VMEM((d, tf), BF16), pltpu.VMEM((d, tf), BF16)], (xb, w_gate, w_up))
    return run(_moe_down_kernel, "moe_expert_down", d // tn,
               [rows_of(ff), w_cols(ff, tn), rows_of(1)], own(tn),
               jax.ShapeDtypeStruct((rows, d), F32),
               [pltpu.VMEM((ff, tn), BF16)], (hidden, w_down, row_w))


def _route(logits, n_groups, n_experts):
    per = n_experts // n_groups
    n = logits.shape[0]
    lg = logits[:, :n_groups]
    le = logits[:, n_groups:n_groups + n_experts]
    gsm = jax.nn.softmax(lg, axis=-1)
    gidx = jnp.argmax(gsm, axis=-1)
    gval = jnp.max(gsm, axis=-1)
    le = jnp.take_along_axis(le.reshape(n, n_groups, per), gidx[:, None, None], axis=1)[:, 0]
    assert TOP_K == 2
    i1 = jnp.argmax(le, axis=-1)
    rest = jnp.where(jnp.arange(per)[None, :] == i1[:, None], -jnp.inf, le)
    i2 = jnp.argmax(rest, axis=-1)
    ev = jnp.stack([jnp.max(le, axis=-1), jnp.max(rest, axis=-1)], axis=-1)
    wts = gval[:, None] * jax.nn.softmax(ev, axis=-1)
    eid = (gidx[:, None] * per + jnp.stack([i1, i2], axis=-1)).astype(jnp.int32)
    return eid, wts


def _dispatch(eid, n_experts, n_rows):
    flat = eid.reshape(-1)
    onehot = (flat[:, None] == jnp.arange(n_experts, dtype=jnp.int32)[None, :]).astype(jnp.int32)
    rank = jnp.take_along_axis(jnp.cumsum(onehot, axis=0) - onehot, flat[:, None], axis=1)[:, 0]
    counts = jnp.sum(onehot, axis=0)
    blocks = (counts + MOE_ROWS - 1) // MOE_ROWS
    bend = jnp.cumsum(blocks)
    bstart = bend - blocks
    slot = bstart[flat] * MOE_ROWS + rank
    n_used = bend[-1:].astype(jnp.int32)
    nblk = n_rows // MOE_ROWS
    block_e = jnp.searchsorted(bend, jnp.arange(nblk, dtype=jnp.int32), side='right')
    block_e = jnp.clip(block_e, 0, n_experts - 1).astype(jnp.int32)
    return slot.astype(jnp.int32), block_e, blocks.astype(jnp.int32), bstart.astype(jnp.int32), n_used


def _moe(x1, n_first, norm_ffn, w_group, b_group, w_router, b_router, w_gate, w_up, w_down):
    n, d = x1.shape
    n_groups = w_group.shape[1]
    n_experts = w_router.shape[1]
    used = n_groups + n_experts
    w_route = jnp.pad(jnp.concatenate([w_group, w_router], axis=1), ((0, 0), (0, LANES - used)))
    b_route = jnp.pad(jnp.concatenate([b_group, b_router]), (0, LANES - used))
    xf, logits = _ffn_norm_router(x1, norm_ffn, w_route, b_route)
    eid, wts = _route(logits, n_groups, n_experts)
    n_assign = n * TOP_K
    n_rows = (n_assign // MOE_ROWS + n_experts) * MOE_ROWS
    slot, block_e, blocks, bstart, n_used = _dispatch(eid, n_experts, n_rows)
    tok = jnp.repeat(jnp.arange(n, dtype=jnp.int32), TOP_K)
    row_tok = (jnp.arange(n_rows, dtype=jnp.int32) % n).at[slot].set(tok)
    row_w = jnp.zeros((n_rows,), F32).at[slot].set(wts.reshape(-1))
    xb = _gather_rows(xf, row_tok, n_used, BF16)
    yb = _moe_ffn(xb, row_w.reshape(n_rows, 1), block_e, blocks, bstart, n_used, w_gate, w_up, w_down)
    return _combine(x1, yb, slot.reshape(n, TOP_K), n_first)


def _layer(xp, xs, conv_s, c_s, n_s, m_s, cache_k, cache_v, page_table, lambda_init, p):
    (norm_mix, w_in, b_gate, conv_w, conv_b, q_norm, k_norm, da_lambda, da_subln,
     m_outnorm, w_out, norm_ffn, w_group, b_group, w_router, b_router, w_gate, w_up, w_down) = p
    bp, tp, d = xp.shape
    bs, ts, _ = xs.shape
    n_p, n_s_rows = bp * tp, bs * ts
    heads_m, dk, dv = c_s.shape[1:]
    da_heads = cache_k.shape[2]
    da_w = da_heads * LANES
    m_w = heads_m * dv
    conv_c = conv_w.shape[1]
    assert da_w == m_w == conv_c, "the six projection regions are assumed equally wide"
    region = da_w
    n_reg = 6
    lp = da_lambda.astype(F32)
    lam = jnp.exp(jnp.sum(lp[0] * lp[1])) - jnp.exp(jnp.sum(lp[2] * lp[3])) + lambda_init
    out_scale = 1.0 - lambda_init

    xp2, xs2 = xp.reshape(n_p, d), xs.reshape(n_s_rows, d)
    w_g = jnp.pad(w_in[:, n_reg * region:], ((0, 0), (0, LANES - 2 * heads_m))).astype(BF16)
    proj, gates = _input_projection(xp2, xs2, norm_mix, w_in.astype(BF16), w_g, region, n_reg)
    proj_s = proj[:, n_p:]

    k_f32, q2, k2, v2 = _qk_prep(proj, q_norm, k_norm, tp)
    o_da_p = _prompt_attention(q2, k2, v2, bp, tp, lam, da_subln, out_scale)
    sample3 = lambda a: a[n_p:].reshape(bs, ts, 2 * da_w)
    o_da_s = _paged_attention(sample3(q2), sample3(k2), sample3(v2), cache_k, cache_v, page_table,
                              lam, da_subln, out_scale)

    q_scale = dk ** -0.5
    zeros_state = jnp.zeros((bp, SUBLANES, conv_c), F32)
    mq_p, mk_p = _conv_silu(proj, 3, bp, tp, zeros_state, conv_w, conv_b, q_scale)
    assert ts <= SUBLANES
    qk_s = proj_s[3].reshape(bs, ts, conv_c)
    xs16 = jnp.concatenate([jnp.zeros((bs, 2 * SUBLANES - ts - (CONV_W - 1), conv_c), F32), conv_s, qk_s], axis=1)
    xs16 = xs16.reshape(1, bs * 2 * SUBLANES, conv_c)
    mq_s, mk_s = _conv_silu(xs16, 0, bs, 2 * SUBLANES, jnp.zeros((bs, SUBLANES, conv_c), F32),
                            conv_w, conv_b, q_scale)
    pad8 = lambda a: jnp.pad(a, ((0, 0), (0, SUBLANES - ts), (0, 0)))
    mq_s = pad8(mq_s[:, 2 * SUBLANES - ts:]).astype(F32)
    mk_s = pad8(mk_s[:, 2 * SUBLANES - ts:]).astype(F32)

    def region_spec(r, nb_rows):
        return lambda rows, nc: pl.BlockSpec((1, rows, region), lambda bi, ci: (r, bi * nc + ci, 0))

    zc = jnp.zeros((bp, heads_m, dk, dv), F32)
    o_m_p, c_p, n_pp, m_p = _mlstm(mq_p, mk_p, proj, region_spec(4, n_p), proj, region_spec(5, n_p),
                                   gates[:n_p].reshape(bp, tp, LANES), b_gate, m_outnorm,
                                   zc, jnp.zeros((bp, heads_m, dk), F32), jnp.zeros((bp, heads_m), F32), M_CHUNK)
    seq_spec = lambda rows, nc: pl.BlockSpec((1, rows, region), lambda bi, ci: (bi, ci, 0))
    v_s8 = pad8(proj_s[4].reshape(bs, ts, region))
    og_s8 = pad8(proj_s[5].reshape(bs, ts, region))
    g_s8 = pad8(gates[n_p:].reshape(bs, ts, LANES))
    o_m_s, c_new, n_new, m_new = _mlstm(mq_s, mk_s, v_s8, seq_spec, og_s8, seq_spec, g_s8, b_gate, m_outnorm,
                                        c_s, n_s, m_s, ts)

    o_da = jnp.concatenate([o_da_p, o_da_s.reshape(n_s_rows, da_w).astype(BF16)], axis=0)
    o_m = jnp.concatenate([o_m_p.reshape(n_p, m_w), o_m_s[:, :ts].reshape(n_s_rows, m_w).astype(BF16)], axis=0)
    x1 = _output_projection(o_da, o_m, w_out.astype(BF16), xp2, xs2)
    y_p, y_s = _moe(x1, n_p, norm_ffn, w_group, b_group, w_router, b_router, w_gate, w_up, w_down)

    k4 = lambda a, b_, t_: a.reshape(b_, t_, da_heads, LANES)
    conv_tail = lambda a, b_, t_: a.reshape(b_, t_, conv_c)[:, t_ - (CONV_W - 1):]
    conv_new_s = jnp.concatenate([conv_s, qk_s], axis=1)[:, ts:]
    outs_p = (k4(k_f32[:n_p], bp, tp), k4(proj[2, :n_p], bp, tp), c_p, n_pp, m_p, conv_tail(proj[3, :n_p], bp, tp))
    outs_s = (k4(k_f32[n_p:], bs, ts), k4(proj_s[2], bs, ts), c_new, n_new, m_new, conv_new_s)
    return y_p.reshape(bp, tp, d), y_s.reshape(bs, ts, d), outs_p, outs_s


def kernel(x_prompt, x_sample, cache_k, cache_v, state_C, state_n, state_m, state_conv, page_table, norm_mix, w_in, b_gate, conv_w, conv_b, q_norm, k_norm, da_lambda, da_subln, m_outnorm, w_out, norm_ffn, w_group, b_group, w_router, b_router, w_gate, w_up, w_down):
    depth = w_in.shape[0]
    yp, ys = x_prompt, x_sample
    outs = [[] for _ in range(12)]
    for l in range(depth):
        lambda_init = 0.8 - 0.6 * math.exp(-0.3 * l)
        p = (norm_mix[l], w_in[l], b_gate[l], conv_w[l], conv_b[l], q_norm[l], k_norm[l], da_lambda[l],
             da_subln[l], m_outnorm[l], w_out[l], norm_ffn[l], w_group[l], b_group[l], w_router[l],
             b_router[l], w_gate[l], w_up[l], w_down[l])
        yp, ys, outs_p, outs_s = _layer(yp, ys, state_conv[l], state_C[l], state_n[l], state_m[l],
                                        cache_k[l], cache_v[l], page_table, lambda_init, p)
        for i, a in enumerate(outs_p + outs_s):
            outs[i].append(a)
    return (yp, ys) + tuple(jnp.stack(o) for o in outs)
```

```python
import functools
import math

import numpy as np
import jax
import jax.numpy as jnp
from jax import lax
from jax.experimental import pallas as pl
from jax.experimental.pallas import tpu as pltpu

F32 = jnp.float32
BF16 = jnp.bfloat16

EPS = 1e-6
ALIBI_MAX = 8.0
CONV_W = 4
TOP_K = 2
M_CHUNK = 128
SHORT_SEQS_PER_STEP = 1
NEG = -1e30

LANES = 128
SUBLANES = 8
VMEM_BYTES_V7X = 64 * 1024 * 1024
VMEM_LIMIT = VMEM_BYTES_V7X - 8 * 1024 * 1024


def _row_tile(n, cap):
    t = cap
    while t > SUBLANES and n % t:
        t //= 2
    assert n % t == 0, (n, t)
    return t


def _params(sem, vmem=None):
    return pltpu.CompilerParams(dimension_semantics=sem, vmem_limit_bytes=vmem or VMEM_LIMIT)


def _two_group_specs(tm, width, nb_first, n_col=1, **spec_kwargs):
    first = pl.BlockSpec((tm, width), lambda i, j: (jnp.minimum(i, nb_first - 1),
                                                    jnp.where(i < nb_first, j, n_col - 1) if n_col > 1 else 0),
                         **spec_kwargs)
    second = pl.BlockSpec((tm, width), lambda i, j: (jnp.maximum(i - nb_first, 0),
                                                     jnp.where(i >= nb_first, j, 0) if n_col > 1 else 0),
                          **spec_kwargs)
    return first, second


def _proj_kernel(xa_ref, xb_ref, g_ref, w_ref, wg_ref, o_ref, og_ref, xn_ref, *, nb_first):
    def normalise(x_ref):
        x = x_ref[...]
        ms = jnp.mean(x * x, axis=-1, keepdims=True)
        xn = ((x * lax.rsqrt(ms + EPS)) * g_ref[...]).astype(BF16)
        xn_ref[...] = xn
        og_ref[...] = jnp.dot(xn, wg_ref[...], preferred_element_type=F32)

    i = pl.program_id(0)
    first_col = pl.program_id(1) == 0
    pl.when(first_col & (i < nb_first))(lambda: normalise(xa_ref))
    pl.when(first_col & (i >= nb_first))(lambda: normalise(xb_ref))
    o_ref[0] = jnp.dot(xn_ref[...], w_ref[...], preferred_element_type=F32)


def _input_projection(xa, xb, g, w, wg, region, n_reg):
    d = xa.shape[1]
    n = xa.shape[0] + xb.shape[0]
    e = n_reg * region
    tm = _row_tile(math.gcd(xa.shape[0], xb.shape[0]), 512)
    nb_first = xa.shape[0] // tm
    tn = min(1024, region)
    per = region // tn
    return pl.pallas_call(
        functools.partial(_proj_kernel, nb_first=nb_first),
        grid=(n // tm, e // tn),
        in_specs=[
            *_two_group_specs(tm, d, nb_first, pipeline_mode=pl.Buffered(1)),
            pl.BlockSpec((1, d), lambda i, j: (0, 0)),
            pl.BlockSpec((d, tn), lambda i, j: (0, j)),
            pl.BlockSpec((d, LANES), lambda i, j: (0, 0)),
        ],
        out_specs=[
            pl.BlockSpec((1, tm, tn), lambda i, j: (j // per, i, j % per)),
            pl.BlockSpec((tm, LANES), lambda i, j: (i, 0)),
        ],
        out_shape=[
            jax.ShapeDtypeStruct((n_reg, n, region), F32),
            jax.ShapeDtypeStruct((n, LANES), F32),
        ],
        scratch_shapes=[pltpu.VMEM((tm, d), BF16)],
        compiler_params=_params(("arbitrary", "arbitrary")),
        name="input_projection",
    )(xa, xb, g.reshape(1, d), w, wg)


SLOPE_TERMS = 4
FEAT = 2 * SLOPE_TERMS
POS_SPLIT = 64
LOG2E = math.log2(math.e)


def _alibi_slopes(n):
    return np.array([2.0 ** (-ALIBI_MAX * (h + 1) / n) for h in range(n)], np.float32)


def _slope_terms(heads):
    import ml_dtypes
    rest = _alibi_slopes(heads).astype(np.float64) * LOG2E
    terms = []
    for _ in range(SLOPE_TERMS):
        t = rest.astype(ml_dtypes.bfloat16).astype(np.float64)
        terms.append(t)
        rest = rest - t
    return np.stack(terms, axis=-1).astype(np.float32)


def _qkprep_kernel(q_ref, k_ref, v_ref, gq_ref, gk_ref, grp_ref, sf_ref, ko_ref, q2_ref, k2_ref, v2_ref,
                   *, heads, half, q_scale, tm, t_seq):
    grp = grp_ref[...]
    lane = lax.broadcasted_iota(jnp.int32, (1, LANES), 1)
    lo_half = lane < half
    fl = lane & (half - 1)
    pos = (pl.program_id(0) * tm) % t_seq + lax.broadcasted_iota(jnp.int32, (tm, 1), 0)
    pos_hi = (pos - (pos & (POS_SPLIT - 1))).astype(F32)
    pos_lo = (pos & (POS_SPLIT - 1)).astype(F32)
    kfeat = jnp.where(fl < FEAT, jnp.where((fl & 1) == 0, pos_hi, pos_lo), 0.0)
    ones_col = jnp.broadcast_to(jnp.where(lane == 0, 1.0, 0.0), (tm, LANES)).astype(BF16)

    def half_norm(x, g):
        sq = x * x
        hi = sq.astype(BF16)
        lo = (sq - hi.astype(F32)).astype(BF16)
        s = jnp.dot(hi, grp, preferred_element_type=F32) + jnp.dot(lo, grp, preferred_element_type=F32)
        return (x * lax.rsqrt(s * (1.0 / half) + EPS)) * g

    for h in range(heads):
        sl = slice(h * LANES, (h + 1) * LANES)
        m0 = slice(2 * h * LANES, (2 * h + 1) * LANES)
        m1 = slice((2 * h + 1) * LANES, (2 * h + 2) * LANES)
        qn = half_norm(q_ref[0, :, sl], gq_ref[...]) * q_scale
        sf = sf_ref[h:h + 1, :]
        q2_ref[:, m0] = jnp.where(lo_half, qn, sf).astype(BF16)
        q2_ref[:, m1] = jnp.where(lo_half, sf, qn).astype(BF16)
        kn = half_norm(k_ref[0, :, sl], gk_ref[...])
        ko_ref[:, sl] = kn
        k2_ref[:, m0] = jnp.where(lo_half, kn, kfeat).astype(BF16)
        k2_ref[:, m1] = jnp.where(lo_half, kfeat, kn).astype(BF16)
        v2_ref[:, m0] = v_ref[0, :, sl].astype(BF16)
        v2_ref[:, m1] = ones_col


def _qk_prep(proj, q_norm, k_norm, t_seq):
    _, n, w = proj.shape
    heads = w // LANES
    half = q_norm.shape[0]
    assert 2 * half == LANES and FEAT <= half
    tm = _row_tile(math.gcd(n, t_seq), 256)
    assert t_seq <= POS_SPLIT * 256, "key positions must split into two bf16-exact factors"
    lane = np.arange(LANES)
    grp = jnp.asarray((lane[:, None] // half == lane[None, :] // half).astype(np.float32), BF16)
    gq = jnp.concatenate([q_norm, q_norm]).reshape(1, LANES)
    gk = jnp.concatenate([k_norm, k_norm]).reshape(1, LANES)
    terms = _slope_terms(heads)
    fl = lane % half
    sf = jnp.asarray(np.where(fl[None, :] < FEAT, terms[:, np.minimum(fl // 2, SLOPE_TERMS - 1)], 0.0))
    row = lambda r: pl.BlockSpec((1, tm, w), lambda i: (r, i, 0))
    full = lambda shape: pl.BlockSpec(shape, lambda i: (0, 0))
    out1 = pl.BlockSpec((tm, w), lambda i: (i, 0))
    out2 = pl.BlockSpec((tm, 2 * w), lambda i: (i, 0))
    return pl.pallas_call(
        functools.partial(_qkprep_kernel, heads=heads, half=half, q_scale=half ** -0.5 * LOG2E,
                          tm=tm, t_seq=t_seq),
        grid=(n // tm,),
        in_specs=[row(0), row(1), row(2), full((1, LANES)), full((1, LANES)), full((LANES, LANES)),
                  full((heads, LANES))],
        out_specs=[out1, out2, out2, out2],
        out_shape=[
            jax.ShapeDtypeStruct((n, w), F32),
            jax.ShapeDtypeStruct((n, 2 * w), BF16),
            jax.ShapeDtypeStruct((n, 2 * w), BF16),
            jax.ShapeDtypeStruct((n, 2 * w), BF16),
        ],
        compiler_params=_params(("arbitrary",)),
        name="qk_prep",
    )(proj, proj, proj, gq, gk, grp, sf)


def _subln(o, g, out_scale):
    ms = jnp.mean(o * o, axis=-1, keepdims=True)
    return ((o * lax.rsqrt(ms + EPS)) * g) * out_scale


def _attn_kernel(lam_ref, q_ref, k_ref, v_ref, gs_ref, o_ref, m0_ref, a0_ref, m1_ref, a1_ref,
                 *, tq, out_scale):
    i = pl.program_id(2)
    q = q_ref[...]
    qs = (q[:, :LANES], q[:, LANES:])
    states = ((m0_ref, a0_ref), (m1_ref, a1_ref))
    for m_ref, a_ref in states:
        m_ref[...] = jnp.full(m_ref.shape, NEG, F32)
        a_ref[...] = jnp.zeros(a_ref.shape, F32)
    causal = lax.broadcasted_iota(jnp.int32, (tq, tq), 0) >= lax.broadcasted_iota(jnp.int32, (tq, tq), 1)

    def scores(j):
        kt = k_ref[pl.ds(pl.multiple_of(j * tq, tq), tq), :]
        return tuple(lax.dot_general(qs[c], kt[:, c * LANES:(c + 1) * LANES], (((1,), (1,)), ((), ())),
                                     preferred_element_type=F32) for c in range(2))

    def accumulate(j, s_maps, masked):
        vt = v_ref[pl.ds(pl.multiple_of(j * tq, tq), tq), :]
        for s, (m_ref, a_ref) in zip(s_maps, states):
            if masked:
                s = jnp.where(causal, s, NEG)
            m_old = m_ref[...]
            m_new = jnp.maximum(m_old, jnp.max(s, axis=-1, keepdims=True))
            p = jnp.exp2(s - m_new)
            a_ref[...] = jnp.exp2(m_old - m_new) * a_ref[...] + jnp.dot(p.astype(BF16), vt,
                                                                         preferred_element_type=F32)
            m_ref[...] = m_new

    def body(j, s_maps):
        s_next = scores(j + 1)
        accumulate(j, s_maps, False)
        return s_next

    accumulate(i, lax.fori_loop(0, i, body, scores(0)), True)
    a0 = a0_ref[...]
    a1 = a1_ref[...]
    o = a0[:, :LANES] / a0[:, LANES:LANES + 1] - lam_ref[0] * (a1[:, :LANES] / a1[:, LANES:LANES + 1])
    o_ref[...] = _subln(o, gs_ref[...], out_scale).astype(o_ref.dtype)


def _prompt_attention(q2, k2, v2, n_batch, t, lam, subln, out_scale):
    heads = q2.shape[1] // (2 * LANES)
    tq = _row_tile(t, 512)
    nq = t // tq
    kv = pl.BlockSpec((t, 2 * LANES), lambda bi, hi, i, lm: (bi, hi))
    return pl.pallas_call(
        functools.partial(_attn_kernel, tq=tq, out_scale=out_scale),
        grid_spec=pltpu.PrefetchScalarGridSpec(
            num_scalar_prefetch=1,
            grid=(n_batch, heads, nq),
            in_specs=[pl.BlockSpec((tq, 2 * LANES), lambda bi, hi, i, lm: (bi * nq + i, hi)), kv, kv,
                      pl.BlockSpec((1, LANES), lambda bi, hi, i, lm: (0, 0))],
            out_specs=pl.BlockSpec((tq, LANES), lambda bi, hi, i, lm: (bi * nq + i, hi)),
            scratch_shapes=[pltpu.VMEM((tq, 1), F32), pltpu.VMEM((tq, 2 * LANES), F32)] * 2,
        ),
        out_shape=jax.ShapeDtypeStruct((n_batch * t, heads * LANES), BF16),
        compiler_params=_params(("arbitrary", "arbitrary", "arbitrary")),
        name="prompt_diff_attention",
    )(lam.reshape(1), q2, k2, v2, subln.reshape(1, LANES))


QSLOTS = 4
NEW_SLOTS = 8


def _paged_attn_kernel(pt_ref, lam_ref, *refs, pages, page, out_scale):
    kp_refs = refs[:pages]
    vp_refs = refs[pages:2 * pages]
    (q_ref, kn_ref, vn_ref, bias_ref, biasn_ref, slope_ref, gs_ref, o_ref, m_ref, l_ref, a_ref) = refs[2 * pages:]
    g = pl.program_id(1)
    q = q_ref[0]
    slope_c = slope_ref[...]
    nt = (((1,), (1,)), ((), ()))

    def update(tiles, first):
        m_old = jnp.full(m_ref.shape, NEG, F32) if first else m_ref[...]
        m_new = m_old
        for s, shift_c, _ in tiles:
            m_new = jnp.maximum(m_new, jnp.max(s, axis=-1, keepdims=True) + shift_c)
        psum = None
        pv = None
        for s, shift_c, v_tile in tiles:
            p = jnp.exp2(s - (m_new - shift_c))
            ps = jnp.sum(p, axis=-1, keepdims=True)
            d = jnp.dot(p.astype(BF16), v_tile, preferred_element_type=F32)
            psum = ps if psum is None else psum + ps
            pv = d if pv is None else pv + d
        if first:
            l_ref[...] = psum
            a_ref[...] = pv
        else:
            alpha = jnp.exp2(m_old - m_new)
            l_ref[...] = alpha * l_ref[...] + psum
            a_ref[...] = alpha * a_ref[...] + pv
        m_ref[...] = m_new

    @pl.when(g == 0)
    def _():
        s = lax.dot_general(q, kn_ref[0], nt, preferred_element_type=F32) + biasn_ref[...]
        update([(s, jnp.zeros_like(slope_c), vn_ref[0])], True)

    tiles = []
    for p in range(pages):
        kt = kp_refs[p][0].astype(BF16)
        s = lax.dot_general(q, kt, nt, preferred_element_type=F32) + bias_ref[...]
        first_pos = ((g * pages + p) * page).astype(F32)
        tiles.append((s, slope_c * first_pos, vp_refs[p][0].astype(BF16)))
    update(tiles, False)

    @pl.when(g == pl.num_programs(1) - 1)
    def _():
        a = a_ref[...] / l_ref[...]
        half_rows = a.shape[0] // 2
        o = a[:half_rows] - lam_ref[0] * a[half_rows:]
        o_ref[0] = _subln(o, gs_ref[...], out_scale)


def _paged_attention(q2, k2, v2, cache_k, cache_v, page_table, lam, subln, out_scale):
    b, tq, _ = q2.shape
    n_pool, page, heads, _ = cache_k.shape
    half = LANES // 2
    n_pages = page_table.shape[1]
    past_len = n_pages * page
    pages = math.gcd(n_pages, 8)
    rows = 2 * heads * QSLOTS
    assert tq <= QSLOTS and tq <= NEW_SLOTS
    q5 = q2.reshape(b, tq, heads, 2, LANES)
    zero = jnp.zeros((b, tq, heads, half), BF16)
    qmaps = jnp.stack([jnp.concatenate([q5[:, :, :, 0, :half], zero], axis=-1),
                       jnp.concatenate([zero, q5[:, :, :, 1, half:]], axis=-1)], axis=1)
    qrows = jnp.pad(qmaps.transpose(0, 1, 3, 2, 4), ((0, 0), (0, 0), (0, 0), (0, QSLOTS - tq), (0, 0)))
    qrows = qrows.reshape(b, rows, LANES)
    k5 = k2.reshape(b, tq, heads, 2, LANES)
    kn = jnp.concatenate([k5[:, :, :, 0, :half], k5[:, :, :, 1, half:]], axis=-1)
    vn = v2.reshape(b, tq, heads, 2, LANES)[:, :, :, 0, :]
    pad_t = ((0, 0), (0, NEW_SLOTS - tq), (0, 0), (0, 0))
    kn = jnp.pad(kn, pad_t).reshape(b, NEW_SLOTS * heads, LANES)
    vn = jnp.pad(vn, pad_t).reshape(b, NEW_SLOTS * heads, LANES)
    slope2 = _slope_terms(heads).astype(np.float64).sum(axis=-1)
    r = np.arange(rows)
    row_head = (r // QSLOTS) % heads
    row_q = r % QSLOTS

    def bias_matrix(n_tok, key_pos0):
        col = np.arange(n_tok * heads)
        col_tok, col_head = col // heads, col % heads
        dist = (past_len + row_q)[:, None] - (key_pos0 + col_tok)[None, :]
        ok = (row_head[:, None] == col_head[None, :]) & (dist >= 0)
        return jnp.asarray(np.where(ok, -slope2[row_head][:, None] * dist, NEG).astype(np.float32))

    bias_page = bias_matrix(page, 0)
    new_tok = np.arange(NEW_SLOTS * heads) // heads
    bias_new = jnp.where(jnp.asarray(new_tok < tq)[None, :], bias_matrix(NEW_SLOTS, past_len), NEG)
    slope_col = jnp.asarray(slope2[row_head].astype(np.float32).reshape(rows, 1))
    ck = cache_k.reshape(n_pool, page * heads, LANES)
    cv = cache_v.reshape(n_pool, page * heads, LANES)

    def page_spec(p):
        return pl.BlockSpec((1, page * heads, LANES),
                            lambda bi, g, pt, lm: (pt[bi * n_pages + g * pages + p], 0, 0))

    per_b = lambda shape: pl.BlockSpec(shape, lambda bi, g, pt, lm: (bi,) + (0,) * (len(shape) - 1))
    const = lambda shape: pl.BlockSpec(shape, lambda bi, g, pt, lm: (0,) * len(shape))
    out = pl.pallas_call(
        functools.partial(_paged_attn_kernel, pages=pages, page=page, out_scale=out_scale),
        grid_spec=pltpu.PrefetchScalarGridSpec(
            num_scalar_prefetch=2,
            grid=(b, n_pages // pages),
            in_specs=[page_spec(p) for p in range(pages)] * 2 + [
                per_b((1, rows, LANES)), per_b((1, NEW_SLOTS * heads, LANES)), per_b((1, NEW_SLOTS * heads, LANES)),
                const((rows, page * heads)), const((rows, NEW_SLOTS * heads)), const((rows, 1)),
                const((1, LANES))],
            out_specs=per_b((1, rows // 2, LANES)),
            scratch_shapes=[pltpu.VMEM((rows, 1), F32), pltpu.VMEM((rows, 1), F32),
                            pltpu.VMEM((rows, LANES), F32)],
        ),
        out_shape=jax.ShapeDtypeStruct((b, rows // 2, LANES), F32),
        compiler_params=_params(("arbitrary", "arbitrary")),
        name="paged_diff_attention",
    )(page_table.reshape(-1), lam.reshape(1), *([ck] * pages), *([cv] * pages),
      qrows, kn, vn, bias_page, bias_new, slope_col, subln.reshape(1, LANES))
    return out.reshape(b, heads, QSLOTS, LANES)[:, :, :tq].transpose(0, 2, 1, 3).reshape(b, tq, heads * LANES)


def _conv_kernel(x_ref, st_ref, w_ref, b_ref, qo_ref, ko_ref, buf_ref, *, tt, q_scale, carry):
    @pl.when(pl.program_id(1) == 0)
    def _():
        buf_ref[0:SUBLANES, :] = st_ref[0]

    x = x_ref[0]
    buf_ref[SUBLANES:SUBLANES + tt, :] = x
    acc = b_ref[...] + w_ref[CONV_W - 1:CONV_W, :] * x
    for k in range(1, CONV_W):
        acc = acc + w_ref[CONV_W - 1 - k:CONV_W - k, :] * buf_ref[SUBLANES - k:SUBLANES - k + tt, :]
    u = acc * jax.nn.sigmoid(acc)
    c = u.shape[1] // 2
    qo_ref[0] = (u[:, :c] * q_scale).astype(qo_ref.dtype)
    ko_ref[0] = u[:, c:].astype(ko_ref.dtype)
    if carry:
        buf_ref[0:SUBLANES, :] = buf_ref[tt:tt + SUBLANES, :]


def _conv_silu(x, region, n_batch, t, state8, conv_w, conv_b, q_scale):
    c = x.shape[2]
    tt = _row_tile(t, 256)
    nt = t // tt
    w8 = jnp.pad(conv_w, ((0, SUBLANES - CONV_W), (0, 0)))
    out = pl.BlockSpec((1, tt, c // 2), lambda bi, i: (bi, i, 0))
    return pl.pallas_call(
        functools.partial(_conv_kernel, tt=tt, q_scale=q_scale, carry=nt > 1),
        grid=(n_batch, nt),
        in_specs=[
            pl.BlockSpec((1, tt, c), lambda bi, i: (region, bi * nt + i, 0)),
            pl.BlockSpec((1, SUBLANES, c), lambda bi, i: (bi, 0, 0)),
            pl.BlockSpec((SUBLANES, c), lambda bi, i: (0, 0)),
            pl.BlockSpec((1, c), lambda bi, i: (0, 0)),
        ],
        out_specs=[out, out],
        out_shape=[jax.ShapeDtypeStruct((n_batch, t, c // 2), BF16)] * 2,
        scratch_shapes=[pltpu.VMEM((tt + SUBLANES, c), F32)],
        compiler_params=_params(("arbitrary", "arbitrary")),
        name="mlstm_conv_silu",
    )(x, state8, w8, conv_b.reshape(1, c))


def _mlstm_kernel(q_ref, k_ref, v_ref, og_ref, g_ref, bg_ref, on_ref, c0_ref, n0_ref, m0_ref,
                  o_ref, c_ref, n_ref, m_ref, cs_ref, ns_ref, ms_ref, *, heads, dk, dv, rows, chunk, t_valid, bb):
    ci = pl.program_id(1)

    @pl.when(ci == 0)
    def _():
        cs_ref[...] = c0_ref[...]
        ns_ref[...] = n0_ref[...]
        ms_ref[...] = m0_ref[...]

    for u in range(bb):
        _mlstm_chunk(u, q_ref, k_ref, v_ref, og_ref, g_ref, bg_ref, on_ref, o_ref, cs_ref, ns_ref, ms_ref,
                     heads=heads, dk=dk, dv=dv, rows=rows, L=chunk, t_valid=t_valid)

    @pl.when(ci == pl.num_programs(1) - 1)
    def _():
        c_ref[...] = cs_ref[...]
        n_ref[...] = ns_ref[...]
        m_ref[...] = ms_ref[...]


def _mlstm_chunk(u, q_ref, k_ref, v_ref, og_ref, g_ref, bg_ref, on_ref, o_ref, cs_ref, ns_ref, ms_ref,
                 *, heads, dk, dv, rows, L, t_valid):
    def pad_rows(a):
        if rows == L:
            return a
        return jnp.concatenate([a, jnp.zeros((L - rows, a.shape[1]), a.dtype)], axis=0)

    gates = pad_rows(g_ref[u] + bg_ref[...])
    lf_all = jnp.minimum(gates, 0.0) - jnp.log1p(jnp.exp(-jnp.abs(gates)))
    q_all = pad_rows(q_ref[u]).astype(BF16)
    k_all = pad_rows(k_ref[u]).astype(BF16)
    v_all = pad_rows(v_ref[u]).astype(BF16)
    r_i = lax.broadcasted_iota(jnp.int32, (L, L), 0)
    c_i = lax.broadcasted_iota(jnp.int32, (L, L), 1)
    eye = r_i == c_i
    tril = c_i <= r_i
    valid = lax.broadcasted_iota(jnp.int32, (L, 1), 0) < t_valid

    def to_row(col):
        return jnp.sum(jnp.where(eye, col, 0.0), axis=0, keepdims=True)

    H = range(heads)
    per_head = lambda f, *lists: [f(*args) for args in zip(*lists)]
    nt = (((1,), (1,)), ((), ()))
    tn = (((0,), (0,)), ((), ()))
    ms_all = ms_ref[u]
    ns_all = ns_ref[u]
    c_old = [cs_ref[u, h] for h in H]
    n_old = [ns_all[h:h + 1, :] for h in H]
    m_prev = [ms_all[h:h + 1, 0:1] for h in H]
    qh = [q_all[:, h * dk:(h + 1) * dk] for h in H]
    kh = [k_all[:, h * dk:(h + 1) * dk] for h in H]
    vh = [v_all[:, h * dv:(h + 1) * dv] for h in H]
    ig_c = [jnp.where(valid, gates[:, h:h + 1], NEG) for h in H]
    lf_c = [jnp.where(valid, lf_all[:, heads + h:heads + h + 1], 0.0) for h in H]
    ig_r = per_head(to_row, ig_c)
    lf_r = per_head(to_row, lf_c)
    b_c = per_head(lambda r: jnp.sum(jnp.where(tril, r, 0.0), axis=1, keepdims=True), lf_r)
    b_r = per_head(to_row, b_c)
    dmat = per_head(lambda bc, br, ir: jnp.where(tril, (bc - br) + ir, NEG), b_c, b_r, ig_r)
    inter = per_head(lambda bc, mp: bc + mp, b_c, m_prev)
    mt = per_head(lambda it, d: jnp.maximum(it, jnp.max(d, axis=1, keepdims=True)), inter, dmat)
    s = per_head(lambda q, k: lax.dot_general(q, k, nt, preferred_element_type=F32), qh, kh)
    wm = per_head(lambda s_, d, m: s_ * jnp.exp(d - m), s, dmat, mt)
    e_in = per_head(lambda it, m: jnp.exp(it - m), inter, mt)
    qc = per_head(lambda q, c: jnp.dot(q, c.astype(BF16), preferred_element_type=F32), qh, c_old)
    wv = per_head(lambda w, v: jnp.dot(w.astype(BF16), v, preferred_element_type=F32), wm, vh)
    num = per_head(lambda e, a, b: e * a + b, e_in, qc, wv)
    den = per_head(lambda e, q, n, w: e * jnp.sum(q.astype(F32) * n, axis=1, keepdims=True)
                   + jnp.sum(w, axis=1, keepdims=True), e_in, qh, n_old, wm)
    hid = per_head(lambda nu, de, m: nu / jnp.maximum(jnp.abs(de), jnp.exp(-m)), num, den, mt)
    ms_h = per_head(lambda x: jnp.mean(x * x, axis=-1, keepdims=True), hid)
    hn = per_head(lambda x, m: (x * lax.rsqrt(m + EPS)) * on_ref[...], hid, ms_h)
    for h in H:
        gate = jax.nn.sigmoid(og_ref[u, :, h * dv:(h + 1) * dv])
        o_ref[u, :, h * dv:(h + 1) * dv] = (gate * hn[h][:rows]).astype(o_ref.dtype)
    b_last = per_head(lambda r: jnp.sum(r, axis=1, keepdims=True), lf_r)
    g_c = per_head(lambda bl, bc, ic: (bl - bc) + ic, b_last, b_c, ig_c)
    g_r = per_head(lambda bl, br, ir: (bl - br) + ir, b_last, b_r, ig_r)
    m_new = per_head(lambda bl, mp, g: jnp.maximum(bl + mp, jnp.max(g, axis=1, keepdims=True)),
                     b_last, m_prev, g_r)
    e_old = per_head(lambda bl, mp, mn: jnp.exp((bl + mp) - mn), b_last, m_prev, m_new)
    kw = per_head(lambda k, g, mn: k.astype(F32) * jnp.exp(g - mn), kh, g_c, m_new)
    kv = per_head(lambda k, v: lax.dot_general(k.astype(BF16), v, tn, preferred_element_type=F32), kw, vh)
    for h in H:
        cs_ref[u, h] = e_old[h] * c_old[h] + kv[h]
    ns_ref[u] = jnp.concatenate(per_head(lambda e, n, k: e * n + jnp.sum(k, axis=0, keepdims=True),
                                         e_old, n_old, kw), axis=0)
    ms_ref[u] = jnp.concatenate([jnp.broadcast_to(m, (1, LANES)) for m in m_new], axis=0)


def _mlstm(q, k, v_src, v_spec, og_src, og_spec, gates, b_gate, outnorm, c0, n0, m0, t_valid):
    b, t, _ = q.shape
    heads, dk, dv = c0.shape[1:]
    rows = min(t, M_CHUNK)
    nc = t // rows
    bg = jnp.pad(b_gate, (0, LANES - b_gate.shape[0])).reshape(1, LANES)
    m0b = jnp.broadcast_to(m0[:, :, None], (b, heads, LANES))
    chunk = M_CHUNK if rows == M_CHUNK else 2 * SUBLANES
    bb = math.gcd(b, SHORT_SEQS_PER_STEP) if rows < M_CHUNK else 1
    blk = lambda w: pl.BlockSpec((bb, rows, w), lambda bi, ci: (bi, ci, 0))
    per_b = lambda shape: pl.BlockSpec((bb,) + shape, lambda bi, ci: (bi,) + (0,) * len(shape))
    const = lambda shape: pl.BlockSpec(shape, lambda bi, ci: (0,) * len(shape))
    kernel = functools.partial(_mlstm_kernel, heads=heads, dk=dk, dv=dv, rows=rows, chunk=chunk,
                               t_valid=t_valid, bb=bb)
    o, c, n, m = pl.pallas_call(
        kernel,
        grid=(b // bb, nc),
        in_specs=[blk(heads * dk), blk(heads * dk), v_spec(bb, rows, nc), og_spec(bb, rows, nc), blk(LANES),
                  const((1, LANES)), const((1, dv)), per_b((heads, dk, dv)), per_b((heads, dk)),
                  per_b((heads, LANES))],
        out_specs=[blk(heads * dv), per_b((heads, dk, dv)), per_b((heads, dk)), per_b((heads, LANES))],
        out_shape=[jax.ShapeDtypeStruct((b, t, heads * dv), BF16 if rows == M_CHUNK else F32),
                   jax.ShapeDtypeStruct((b, heads, dk, dv), F32),
                   jax.ShapeDtypeStruct((b, heads, dk), F32),
                   jax.ShapeDtypeStruct((b, heads, LANES), F32)],
        scratch_shapes=[pltpu.VMEM((bb, heads, dk, dv), F32), pltpu.VMEM((bb, heads, dk), F32),
                        pltpu.VMEM((bb, heads, LANES), F32)],
        compiler_params=_params(("arbitrary", "arbitrary")),
        name="mlstm_chunkwise",
    )(q, k, v_src, og_src, gates, bg, outnorm.reshape(1, dv), c0, n0, m0b)
    return o, c, n, m[:, :, 0]


def _outproj_kernel(a1_ref, a2_ref, b1_ref, b2_ref, wa_ref, wb_ref, x1_ref, x2_ref, o_ref, *, nb_first):
    def project(a_ref, b_ref, x_ref):
        o_ref[...] = x_ref[...] + (jnp.dot(a_ref[...], wa_ref[...], preferred_element_type=F32)
                                   + jnp.dot(b_ref[...], wb_ref[...], preferred_element_type=F32))

    i = pl.program_id(0)
    pl.when(i < nb_first)(lambda: project(a1_ref, b1_ref, x1_ref))
    pl.when(i >= nb_first)(lambda: project(a2_ref, b2_ref, x2_ref))


def _output_projection(a_groups, b_groups, w, x_groups):
    n1, wa = a_groups[0].shape
    n2 = a_groups[1].shape[0]
    wb = b_groups[0].shape[1]
    d = w.shape[1]
    tm = _row_tile(math.gcd(n1, n2), 512)
    nb_first = n1 // tm
    tn = min(1024, d)
    return pl.pallas_call(
        functools.partial(_outproj_kernel, nb_first=nb_first),
        grid=((n1 + n2) // tm, d // tn),
        in_specs=[
            *_two_group_specs(tm, wa, nb_first),
            *_two_group_specs(tm, wb, nb_first),
            pl.BlockSpec((wa, tn), lambda i, j: (0, j)),
            pl.BlockSpec((wb, tn), lambda i, j: (wa // wb, j)),
            *_two_group_specs(tm, tn, nb_first, n_col=d // tn),
        ],
        out_specs=pl.BlockSpec((tm, tn), lambda i, j: (i, j)),
        out_shape=jax.ShapeDtypeStruct((n1 + n2, d), F32),
        compiler_params=_params(("arbitrary", "arbitrary")),
        name="output_projection",
    )(*a_groups, *b_groups, w, w, *x_groups)


def _router_kernel(x_ref, g_ref, whi_ref, wlo_ref, b_ref, xf_ref, lg_ref):
    x = x_ref[...]
    ms = jnp.mean(x * x, axis=-1, keepdims=True)
    xn = (x * lax.rsqrt(ms + EPS)) * g_ref[...]
    hi = xn.astype(BF16)
    lo = (xn - hi.astype(F32)).astype(BF16)
    xf_ref[...] = xn
    lg_ref[...] = b_ref[...] + jnp.dot(hi, whi_ref[...], preferred_element_type=F32) \
        + (jnp.dot(hi, wlo_ref[...], preferred_element_type=F32)
           + jnp.dot(lo, whi_ref[...], preferred_element_type=F32))


def _ffn_norm_router(x, g, w_route, b_route):
    n, d = x.shape
    tm = _row_tile(n, 256)
    whi = w_route.astype(BF16)
    wlo = (w_route - whi.astype(F32)).astype(BF16)
    const = lambda shape: pl.BlockSpec(shape, lambda i: (0, 0))
    return pl.pallas_call(
        _router_kernel,
        grid=(n // tm,),
        in_specs=[pl.BlockSpec((tm, d), lambda i: (i, 0)), const((1, d)), const((d, LANES)),
                  const((d, LANES)), const((1, LANES))],
        out_specs=[pl.BlockSpec((tm, d), lambda i: (i, 0)), pl.BlockSpec((tm, LANES), lambda i: (i, 0))],
        out_shape=[jax.ShapeDtypeStruct((n, d), F32), jax.ShapeDtypeStruct((n, LANES), F32)],
        compiler_params=_params(("arbitrary",)),
        name="ffn_norm_router",
    )(x, g.reshape(1, d), whi, wlo, b_route.reshape(1, LANES))


MOE_ROWS = 256
MOE_FF_TILE = 512
MOE_OUT_TILE = 2048


def _row_copy(src_ref, src_row, dst_ref, dst_row, sem):
    return pltpu.make_async_copy(src_ref.at[pl.ds(src_row, 1)], dst_ref.at[pl.ds(dst_row, 1)], sem)


DMA_UNROLL = 8


def _gather_rows_kernel(nu_ref, ids_ref, src_ref, o_ref, buf_ref, sem, *, rows):
    def start(r, carry):
        _row_copy(src_ref, ids_ref[0, 0, r], buf_ref, r, sem).start()
        return carry

    def wait(r, carry):
        _row_copy(src_ref, 0, buf_ref, r, sem).wait()
        return carry

    @pl.when(pl.program_id(0) < nu_ref[0])
    def _():
        lax.fori_loop(0, rows, start, 0, unroll=DMA_UNROLL)
        lax.fori_loop(0, rows, wait, 0, unroll=DMA_UNROLL)
        o_ref[...] = buf_ref[...].astype(o_ref.dtype)

    @pl.when(pl.program_id(0) >= nu_ref[0])
    def _():
        o_ref[...] = jnp.zeros(o_ref.shape, o_ref.dtype)


def _gather_rows(src, ids, n_used, out_dtype):
    n_rows = ids.shape[0]
    d = src.shape[1]
    nblk = n_rows // MOE_ROWS
    return pl.pallas_call(
        functools.partial(_gather_rows_kernel, rows=MOE_ROWS),
        grid_spec=pltpu.PrefetchScalarGridSpec(
            num_scalar_prefetch=1,
            grid=(nblk,),
            in_specs=[pl.BlockSpec((1, 1, MOE_ROWS), lambda i, nu: (i, 0, 0), memory_space=pltpu.SMEM),
                      pl.BlockSpec(memory_space=pl.ANY)],
            out_specs=pl.BlockSpec((MOE_ROWS, d), lambda i, nu: (i, 0)),
            scratch_shapes=[pltpu.VMEM((MOE_ROWS, d), src.dtype), pltpu.SemaphoreType.DMA(())],
        ),
        out_shape=jax.ShapeDtypeStruct((n_rows, d), out_dtype),
        compiler_params=_params(("arbitrary",)),
        name="moe_gather_rows",
    )(n_used, ids.reshape(nblk, 1, MOE_ROWS), src)


COMBINE_ROWS = 128


def _combine_kernel(slots_ref, x_ref, yb_ref, oa_ref, ob_ref, buf_ref, sem, *, tc, nb_first):
    n_copies = TOP_K * tc

    def start(r, carry):
        _row_copy(yb_ref, slots_ref[0, 0, r], buf_ref, r, sem).start()
        return carry

    def wait(r, carry):
        _row_copy(yb_ref, 0, buf_ref, r, sem).wait()
        return carry

    lax.fori_loop(0, n_copies, start, 0, unroll=DMA_UNROLL)
    lax.fori_loop(0, n_copies, wait, 0, unroll=DMA_UNROLL)
    acc = buf_ref[0:tc, :]
    for k in range(1, TOP_K):
        acc = acc + buf_ref[k * tc:(k + 1) * tc, :]
    i = pl.program_id(0)

    @pl.when(i < nb_first)
    def _():
        oa_ref[...] = x_ref[...] + acc

    @pl.when(i >= nb_first)
    def _():
        ob_ref[...] = x_ref[...] + acc


def _combine(x, yb, slot2, n_first):
    n, d = x.shape
    tc = _row_tile(math.gcd(n_first, n - n_first), COMBINE_ROWS)
    nblk = n // tc
    nb_first = n_first // tc
    slots = slot2.reshape(nblk, tc, TOP_K).transpose(0, 2, 1).reshape(nblk, 1, TOP_K * tc)
    return pl.pallas_call(
        functools.partial(_combine_kernel, tc=tc, nb_first=nb_first),
        grid=(nblk,),
        in_specs=[pl.BlockSpec((1, 1, TOP_K * tc), lambda i: (i, 0, 0), memory_space=pltpu.SMEM),
                  pl.BlockSpec((tc, d), lambda i: (i, 0)),
                  pl.BlockSpec(memory_space=pl.ANY)],
        out_specs=[pl.BlockSpec((tc, d), lambda i: (jnp.minimum(i, nb_first - 1), 0)),
                   pl.BlockSpec((tc, d), lambda i: (jnp.maximum(i - nb_first, 0), 0))],
        out_shape=[jax.ShapeDtypeStruct((n_first, d), F32), jax.ShapeDtypeStruct((n - n_first, d), F32)],
        scratch_shapes=[pltpu.VMEM((TOP_K * tc, d), F32), pltpu.SemaphoreType.DMA(())],
        compiler_params=_params(("arbitrary",)),
        name="moe_combine",
    )(slots, x, yb)


def _weights_changed(t, ib_ref, is_ref, be_ref):
    prev = jnp.maximum(t - 1, 0)
    return (t == 0) | (be_ref[ib_ref[t]] != be_ref[ib_ref[prev]]) | (is_ref[t] != is_ref[prev])


def _moe_up_kernel(ib_ref, is_ref, be_ref, ni_ref, x_ref, wg_ref, wu_ref, h_ref, wgb_ref, wub_ref):
    t = pl.program_id(0)

    @pl.when(t >= ni_ref[0])
    def _():
        h_ref[...] = jnp.zeros(h_ref.shape, h_ref.dtype)

    @pl.when(t < ni_ref[0])
    def _():
        @pl.when(_weights_changed(t, ib_ref, is_ref, be_ref))
        def _():
            wgb_ref[...] = wg_ref[0].astype(BF16)
            wub_ref[...] = wu_ref[0].astype(BF16)

        x = x_ref[...]
        gt = jnp.dot(x, wgb_ref[...], preferred_element_type=F32)
        up = jnp.dot(x, wub_ref[...], preferred_element_type=F32)
        h_ref[...] = ((gt * jax.nn.sigmoid(gt)) * up).astype(h_ref.dtype)


def _moe_down_kernel(ib_ref, is_ref, be_ref, ni_ref, h_ref, wd_ref, rw_ref, o_ref, wdb_ref):
    t = pl.program_id(0)

    @pl.when(t >= ni_ref[0])
    def _():
        o_ref[...] = jnp.zeros(o_ref.shape, F32)

    @pl.when(t < ni_ref[0])
    def _():
        @pl.when(_weights_changed(t, ib_ref, is_ref, be_ref))
        def _():
            wdb_ref[...] = wd_ref[0].astype(BF16)

        o_ref[...] = jnp.dot(h_ref[...], wdb_ref[...], preferred_element_type=F32) * rw_ref[...]


def _item_tables(block_e, blocks, bstart, n_used, nblk, nsplit):
    b = jnp.arange(nblk, dtype=jnp.int32)
    s = jnp.arange(nsplit, dtype=jnp.int32)
    e = block_e
    pos_live = nsplit * bstart[e][:, None] + s[None, :] * blocks[e][:, None] + (b - bstart[e])[:, None]
    pos_idle = b[:, None] * nsplit + s[None, :]
    pos = jnp.where((b < n_used[0])[:, None], pos_live, pos_idle).reshape(-1)
    item_b = jnp.zeros((nblk * nsplit,), jnp.int32).at[pos].set(jnp.repeat(b, nsplit))
    item_s = jnp.zeros((nblk * nsplit,), jnp.int32).at[pos].set(jnp.tile(s, nblk))
    return item_b, item_s, (n_used * nsplit).astype(jnp.int32)


def _moe_ffn(xb, row_w, block_e, blocks, bstart, n_used, w_gate, w_up, w_down):
    rows, d = xb.shape
    ff = w_gate.shape[2]
    tf = min(MOE_FF_TILE, ff)
    tn = min(MOE_OUT_TILE, d)
    nblk = rows // MOE_ROWS

    def live(t, ni):
        return jnp.minimum(t, ni[0] - 1)

    def run(kernel, name, nsplit, in_specs, out_spec, out_shape, scratch, args):
        item_b, item_s, n_items = _item_tables(block_e, blocks, bstart, n_used, nblk, nsplit)
        return pl.pallas_call(
            kernel,
            grid_spec=pltpu.PrefetchScalarGridSpec(
                num_scalar_prefetch=4, grid=(nblk * nsplit,), in_specs=in_specs, out_specs=out_spec,
                scratch_shapes=scratch),
            out_shape=out_shape,
            compiler_params=_params(("arbitrary",)),
            name=name,
        )(item_b, item_s, block_e, n_items, *args)

    rows_of = lambda w: pl.BlockSpec((MOE_ROWS, w), lambda t, ib, s, be, ni: (ib[live(t, ni)], 0))
    w_cols = lambda k, w: pl.BlockSpec(
        (1, k, w), lambda t, ib, s, be, ni: (be[ib[live(t, ni)]], 0, s[live(t, ni)]))
    own = lambda w: pl.BlockSpec((MOE_ROWS, w), lambda t, ib, s, be, ni: (ib[t], s[t]))
    hidden = run(_moe_up_kernel, "moe_expert_up", ff // tf,
                 [rows_of(d), w_cols(d, tf), w_cols(d, tf)], own(tf),
                 jax.ShapeDtypeStruct((rows, ff), BF16),
                 [pltpu.VMEM((d, tf), BF16), pltpu.VMEM((d, tf), BF16)], (xb, w_gate, w_up))
    return run(_moe_down_kernel, "moe_expert_down", d // tn,
               [rows_of(ff), w_cols(ff, tn), rows_of(1)], own(tn),
               jax.ShapeDtypeStruct((rows, d), F32),
               [pltpu.VMEM((ff, tn), BF16)], (hidden, w_down, row_w))


def _route(logits, n_groups, n_experts):
    per = n_experts // n_groups
    n = logits.shape[0]
    lg = logits[:, :n_groups]
    le = logits[:, n_groups:n_groups + n_experts]
    gsm = jax.nn.softmax(lg, axis=-1)
    gidx = jnp.argmax(gsm, axis=-1)
    gval = jnp.max(gsm, axis=-1)
    le = jnp.take_along_axis(le.reshape(n, n_groups, per), gidx[:, None, None], axis=1)[:, 0]
    assert TOP_K == 2
    i1 = jnp.argmax(le, axis=-1)
    rest = jnp.where(jnp.arange(per)[None, :] == i1[:, None], -jnp.inf, le)
    i2 = jnp.argmax(rest, axis=-1)
    ev = jnp.stack([jnp.max(le, axis=-1), jnp.max(rest, axis=-1)], axis=-1)
    wts = gval[:, None] * jax.nn.softmax(ev, axis=-1)
    eid = (gidx[:, None] * per + jnp.stack([i1, i2], axis=-1)).astype(jnp.int32)
    return eid, wts


def _dispatch(eid, n_experts, n_rows):
    flat = eid.reshape(-1)
    onehot = (flat[:, None] == jnp.arange(n_experts, dtype=jnp.int32)[None, :]).astype(jnp.int32)
    rank = jnp.take_along_axis(jnp.cumsum(onehot, axis=0) - onehot, flat[:, None], axis=1)[:, 0]
    counts = jnp.sum(onehot, axis=0)
    blocks = (counts + MOE_ROWS - 1) // MOE_ROWS
    bend = jnp.cumsum(blocks)
    bstart = bend - blocks
    slot = bstart[flat] * MOE_ROWS + rank
    n_used = bend[-1:].astype(jnp.int32)
    nblk = n_rows // MOE_ROWS
    block_e = jnp.searchsorted(bend, jnp.arange(nblk, dtype=jnp.int32), side='right')
    block_e = jnp.clip(block_e, 0, n_experts - 1).astype(jnp.int32)
    return slot.astype(jnp.int32), block_e, blocks.astype(jnp.int32), bstart.astype(jnp.int32), n_used


def _moe(x1, n_first, norm_ffn, w_group, b_group, w_router, b_router, w_gate, w_up, w_down):
    n, d = x1.shape
    n_groups = w_group.shape[1]
    n_experts = w_router.shape[1]
    used = n_groups + n_experts
    w_route = jnp.pad(jnp.concatenate([w_group, w_router], axis=1), ((0, 0), (0, LANES - used)))
    b_route = jnp.pad(jnp.concatenate([b_group, b_router]), (0, LANES - used))
    xf, logits = _ffn_norm_router(x1, norm_ffn, w_route, b_route)
    eid, wts = _route(logits, n_groups, n_experts)
    n_assign = n * TOP_K
    n_rows = (n_assign // MOE_ROWS + n_experts) * MOE_ROWS
    slot, block_e, blocks, bstart, n_used = _dispatch(eid, n_experts, n_rows)
    row_a = jnp.full((n_rows,), -1, jnp.int32).at[slot].set(jnp.arange(n_assign, dtype=jnp.int32))
    row_tok = jnp.where(row_a >= 0, row_a // TOP_K, jnp.arange(n_rows, dtype=jnp.int32) % n)
    row_w = jnp.where(row_a >= 0, wts.reshape(-1)[jnp.maximum(row_a, 0)], 0.0)
    xb = _gather_rows(xf, row_tok, n_used, BF16)
    yb = _moe_ffn(xb, row_w.reshape(n_rows, 1), block_e, blocks, bstart, n_used, w_gate, w_up, w_down)
    return _combine(x1, yb, slot.reshape(n, TOP_K), n_first)


def _layer(xp, xs, conv_s, c_s, n_s, m_s, cache_k, cache_v, page_table, lambda_init, p):
    (norm_mix, w_in, b_gate, conv_w, conv_b, q_norm, k_norm, da_lambda, da_subln,
     m_outnorm, w_out, norm_ffn, w_group, b_group, w_router, b_router, w_gate, w_up, w_down) = p
    bp, tp, d = xp.shape
    bs, ts, _ = xs.shape
    n_p, n_s_rows = bp * tp, bs * ts
    heads_m, dk, dv = c_s.shape[1:]
    da_heads = cache_k.shape[2]
    da_w = da_heads * LANES
    m_w = heads_m * dv
    conv_c = conv_w.shape[1]
    assert da_w == m_w == conv_c, "the six projection regions are assumed equally wide"
    region = da_w
    n_reg = 6
    lp = da_lambda.astype(F32)
    lam = jnp.exp(jnp.sum(lp[0] * lp[1])) - jnp.exp(jnp.sum(lp[2] * lp[3])) + lambda_init
    out_scale = 1.0 - lambda_init

    xp2, xs2 = xp.reshape(n_p, d), xs.reshape(n_s_rows, d)
    w_g = jnp.pad(w_in[:, n_reg * region:], ((0, 0), (0, LANES - 2 * heads_m))).astype(BF16)
    proj, gates = _input_projection(xp2, xs2, norm_mix, w_in.astype(BF16), w_g, region, n_reg)
    proj_s = proj[:, n_p:]

    k_f32, q2, k2, v2 = _qk_prep(proj, q_norm, k_norm, tp)
    o_da_p = _prompt_attention(q2, k2, v2, bp, tp, lam, da_subln, out_scale)
    sample3 = lambda a: a[n_p:].reshape(bs, ts, 2 * da_w)
    o_da_s = _paged_attention(sample3(q2), sample3(k2), sample3(v2), cache_k, cache_v, page_table,
                              lam, da_subln, out_scale)

    q_scale = dk ** -0.5
    zeros_state = jnp.zeros((bp, SUBLANES, conv_c), F32)
    mq_p, mk_p = _conv_silu(proj, 3, bp, tp, zeros_state, conv_w, conv_b, q_scale)
    assert ts <= SUBLANES
    qk_s = proj_s[3].reshape(bs, ts, conv_c)
    xs16 = jnp.concatenate([jnp.zeros((bs, 2 * SUBLANES - ts - (CONV_W - 1), conv_c), F32), conv_s, qk_s], axis=1)
    xs16 = xs16.reshape(1, bs * 2 * SUBLANES, conv_c)
    mq_s, mk_s = _conv_silu(xs16, 0, bs, 2 * SUBLANES, jnp.zeros((bs, SUBLANES, conv_c), F32),
                            conv_w, conv_b, q_scale)
    pad8 = lambda a: jnp.pad(a, ((0, 0), (0, SUBLANES - ts), (0, 0)))
    mq_s = pad8(mq_s[:, 2 * SUBLANES - ts:]).astype(F32)
    mk_s = pad8(mk_s[:, 2 * SUBLANES - ts:]).astype(F32)

    def region_spec(r, nb_rows):
        def spec(bb, rows, nc):
            assert bb == 1
            return pl.BlockSpec((1, rows, region), lambda bi, ci: (r, bi * nc + ci, 0))
        return spec

    zc = jnp.zeros((bp, heads_m, dk, dv), F32)
    o_m_p, c_p, n_pp, m_p = _mlstm(mq_p, mk_p, proj, region_spec(4, n_p), proj, region_spec(5, n_p),
                                   gates[:n_p].reshape(bp, tp, LANES), b_gate, m_outnorm,
                                   zc, jnp.zeros((bp, heads_m, dk), F32), jnp.zeros((bp, heads_m), F32), M_CHUNK)
    seq_spec = lambda bb, rows, nc: pl.BlockSpec((bb, rows, region), lambda bi, ci: (bi, ci, 0))
    v_s8 = pad8(proj_s[4].reshape(bs, ts, region))
    og_s8 = pad8(proj_s[5].reshape(bs, ts, region))
    g_s8 = pad8(gates[n_p:].reshape(bs, ts, LANES))
    o_m_s, c_new, n_new, m_new = _mlstm(mq_s, mk_s, v_s8, seq_spec, og_s8, seq_spec, g_s8, b_gate, m_outnorm,
                                        c_s, n_s, m_s, ts)

    o_da = (o_da_p, o_da_s.reshape(n_s_rows, da_w).astype(BF16))
    o_m = (o_m_p.reshape(n_p, m_w), o_m_s[:, :ts].reshape(n_s_rows, m_w).astype(BF16))
    x1 = _output_projection(o_da, o_m, w_out.astype(BF16), (xp2, xs2))
    y_p, y_s = _moe(x1, n_p, norm_ffn, w_group, b_group, w_router, b_router, w_gate, w_up, w_down)

    k4 = lambda a, b_, t_: a.reshape(b_, t_, da_heads, LANES)
    conv_tail = lambda a, b_, t_: a.reshape(b_, t_, conv_c)[:, t_ - (CONV_W - 1):]
    conv_new_s = jnp.concatenate([conv_s, qk_s], axis=1)[:, ts:]
    outs_p = (k4(k_f32[:n_p], bp, tp), k4(proj[2, :n_p], bp, tp), c_p, n_pp, m_p, conv_tail(proj[3, :n_p], bp, tp))
    outs_s = (k4(k_f32[n_p:], bs, ts), k4(proj_s[2], bs, ts), c_new, n_new, m_new, conv_new_s)
    return y_p.reshape(bp, tp, d), y_s.reshape(bs, ts, d), outs_p, outs_s


def kernel(x_prompt, x_sample, cache_k, cache_v, state_C, state_n, state_m, state_conv, page_table, norm_mix, w_in, b_gate, conv_w, conv_b, q_norm, k_norm, da_lambda, da_subln, m_outnorm, w_out, norm_ffn, w_group, b_group, w_router, b_router, w_gate, w_up, w_down):
    depth = w_in.shape[0]
    yp, ys = x_prompt, x_sample
    outs = [[] for _ in range(12)]
    for l in range(depth):
        lambda_init = 0.8 - 0.6 * math.exp(-0.3 * l)
        p = (norm_mix[l], w_in[l], b_gate[l], conv_w[l], conv_b[l], q_norm[l], k_norm[l], da_lambda[l],
             da_subln[l], m_outnorm[l], w_out[l], norm_ffn[l], w_group[l], b_group[l], w_router[l],
             b_router[l], w_gate[l], w_up[l], w_down[l])
        yp, ys, outs_p, outs_s = _layer(yp, ys, state_conv[l], state_C[l], state_n[l], state_m[l],
                                        cache_k[l], cache_v[l], page_table, lambda_init, p)
        for i, a in enumerate(outs_p + outs_s):
            outs[i].append(a)
    return (yp, ys) + tuple(jnp.stack(o) for o in outs)
```

```python
import functools
import math

import numpy as np
import jax
import jax.numpy as jnp
from jax import lax
from jax.experimental import pallas as pl
from jax.experimental.pallas import tpu as pltpu

F32 = jnp.float32
BF16 = jnp.bfloat16

EPS = 1e-6
ALIBI_MAX = 8.0
CONV_W = 4
TOP_K = 2
M_CHUNK = 128
SHORT_SEQS_PER_STEP = 4
NEG = -1e30

LANES = 128
SUBLANES = 8
VMEM_BYTES_V7X = 64 * 1024 * 1024
VMEM_LIMIT = VMEM_BYTES_V7X - 8 * 1024 * 1024


def _row_tile(n, cap):
    t = cap
    while t > SUBLANES and n % t:
        t //= 2
    assert n % t == 0, (n, t)
    return t


def _params(sem, vmem=None):
    return pltpu.CompilerParams(dimension_semantics=sem, vmem_limit_bytes=vmem or VMEM_LIMIT)


def _two_group_specs(tm, width, nb_first, n_col=1, **spec_kwargs):
    first = pl.BlockSpec((tm, width), lambda i, j: (jnp.minimum(i, nb_first - 1),
                                                    jnp.where(i < nb_first, j, n_col - 1) if n_col > 1 else 0),
                         **spec_kwargs)
    second = pl.BlockSpec((tm, width), lambda i, j: (jnp.maximum(i - nb_first, 0),
                                                     jnp.where(i >= nb_first, j, 0) if n_col > 1 else 0),
                          **spec_kwargs)
    return first, second


def _proj_kernel(xa_ref, xb_ref, g_ref, w_ref, wg_ref, o_ref, og_ref, xn_ref, *, nb_first):
    def normalise(x_ref):
        x = x_ref[...]
        ms = jnp.mean(x * x, axis=-1, keepdims=True)
        xn = ((x * lax.rsqrt(ms + EPS)) * g_ref[...]).astype(BF16)
        xn_ref[...] = xn
        og_ref[...] = jnp.dot(xn, wg_ref[...], preferred_element_type=F32)

    i = pl.program_id(0)
    first_col = pl.program_id(1) == 0
    pl.when(first_col & (i < nb_first))(lambda: normalise(xa_ref))
    pl.when(first_col & (i >= nb_first))(lambda: normalise(xb_ref))
    o_ref[0] = jnp.dot(xn_ref[...], w_ref[...], preferred_element_type=F32)


def _input_projection(xa, xb, g, w, wg, region, n_reg):
    d = xa.shape[1]
    n = xa.shape[0] + xb.shape[0]
    e = n_reg * region
    tm = _row_tile(math.gcd(xa.shape[0], xb.shape[0]), 512)
    nb_first = xa.shape[0] // tm
    tn = min(1024, region)
    per = region // tn
    return pl.pallas_call(
        functools.partial(_proj_kernel, nb_first=nb_first),
        grid=(n // tm, e // tn),
        in_specs=[
            *_two_group_specs(tm, d, nb_first, pipeline_mode=pl.Buffered(1)),
            pl.BlockSpec((1, d), lambda i, j: (0, 0)),
            pl.BlockSpec((d, tn), lambda i, j: (0, j)),
            pl.BlockSpec((d, LANES), lambda i, j: (0, 0)),
        ],
        out_specs=[
            pl.BlockSpec((1, tm, tn), lambda i, j: (j // per, i, j % per)),
            pl.BlockSpec((tm, LANES), lambda i, j: (i, 0)),
        ],
        out_shape=[
            jax.ShapeDtypeStruct((n_reg, n, region), F32),
            jax.ShapeDtypeStruct((n, LANES), F32),
        ],
        scratch_shapes=[pltpu.VMEM((tm, d), BF16)],
        compiler_params=_params(("arbitrary", "arbitrary")),
        name="input_projection",
    )(xa, xb, g.reshape(1, d), w, wg)


SLOPE_TERMS = 4
FEAT = 2 * SLOPE_TERMS
POS_SPLIT = 64
LOG2E = math.log2(math.e)


def _alibi_slopes(n):
    return np.array([2.0 ** (-ALIBI_MAX * (h + 1) / n) for h in range(n)], np.float32)


def _slope_terms(heads):
    import ml_dtypes
    rest = _alibi_slopes(heads).astype(np.float64) * LOG2E
    terms = []
    for _ in range(SLOPE_TERMS):
        t = rest.astype(ml_dtypes.bfloat16).astype(np.float64)
        terms.append(t)
        rest = rest - t
    return np.stack(terms, axis=-1).astype(np.float32)


def _qkprep_kernel(q_ref, k_ref, v_ref, gq_ref, gk_ref, grp_ref, sf_ref, ko_ref, q2_ref, k2_ref, v2_ref,
                   *, heads, half, q_scale, tm, t_seq):
    grp = grp_ref[...]
    lane = lax.broadcasted_iota(jnp.int32, (1, LANES), 1)
    lo_half = lane < half
    fl = lane & (half - 1)
    pos = (pl.program_id(0) * tm) % t_seq + lax.broadcasted_iota(jnp.int32, (tm, 1), 0)
    pos_hi = (pos - (pos & (POS_SPLIT - 1))).astype(F32)
    pos_lo = (pos & (POS_SPLIT - 1)).astype(F32)
    kfeat = jnp.where(fl < FEAT, jnp.where((fl & 1) == 0, pos_hi, pos_lo), 0.0)
    ones_col = jnp.broadcast_to(jnp.where(lane == 0, 1.0, 0.0), (tm, LANES)).astype(BF16)

    def half_norm(x, g):
        sq = x * x
        hi = sq.astype(BF16)
        lo = (sq - hi.astype(F32)).astype(BF16)
        s = jnp.dot(hi, grp, preferred_element_type=F32) + jnp.dot(lo, grp, preferred_element_type=F32)
        return (x * lax.rsqrt(s * (1.0 / half) + EPS)) * g

    for h in range(heads):
        sl = slice(h * LANES, (h + 1) * LANES)
        m0 = slice(2 * h * LANES, (2 * h + 1) * LANES)
        m1 = slice((2 * h + 1) * LANES, (2 * h + 2) * LANES)
        qn = half_norm(q_ref[0, :, sl], gq_ref[...]) * q_scale
        sf = sf_ref[h:h + 1, :]
        q2_ref[:, m0] = jnp.where(lo_half, qn, sf).astype(BF16)
        q2_ref[:, m1] = jnp.where(lo_half, sf, qn).astype(BF16)
        kn = half_norm(k_ref[0, :, sl], gk_ref[...])
        ko_ref[:, sl] = kn
        k2_ref[:, m0] = jnp.where(lo_half, kn, kfeat).astype(BF16)
        k2_ref[:, m1] = jnp.where(lo_half, kfeat, kn).astype(BF16)
        v2_ref[:, m0] = v_ref[0, :, sl].astype(BF16)
        v2_ref[:, m1] = ones_col


def _qk_prep(proj, q_norm, k_norm, t_seq):
    _, n, w = proj.shape
    heads = w // LANES
    half = q_norm.shape[0]
    assert 2 * half == LANES and FEAT <= half
    tm = _row_tile(math.gcd(n, t_seq), 256)
    assert t_seq <= POS_SPLIT * 256, "key positions must split into two bf16-exact factors"
    lane = np.arange(LANES)
    grp = jnp.asarray((lane[:, None] // half == lane[None, :] // half).astype(np.float32), BF16)
    gq = jnp.concatenate([q_norm, q_norm]).reshape(1, LANES)
    gk = jnp.concatenate([k_norm, k_norm]).reshape(1, LANES)
    terms = _slope_terms(heads)
    fl = lane % half
    sf = jnp.asarray(np.where(fl[None, :] < FEAT, terms[:, np.minimum(fl // 2, SLOPE_TERMS - 1)], 0.0))
    row = lambda r: pl.BlockSpec((1, tm, w), lambda i: (r, i, 0))
    full = lambda shape: pl.BlockSpec(shape, lambda i: (0, 0))
    out1 = pl.BlockSpec((tm, w), lambda i: (i, 0))
    out2 = pl.BlockSpec((tm, 2 * w), lambda i: (i, 0))
    return pl.pallas_call(
        functools.partial(_qkprep_kernel, heads=heads, half=half, q_scale=half ** -0.5 * LOG2E,
                          tm=tm, t_seq=t_seq),
        grid=(n // tm,),
        in_specs=[row(0), row(1), row(2), full((1, LANES)), full((1, LANES)), full((LANES, LANES)),
                  full((heads, LANES))],
        out_specs=[out1, out2, out2, out2],
        out_shape=[
            jax.ShapeDtypeStruct((n, w), F32),
            jax.ShapeDtypeStruct((n, 2 * w), BF16),
            jax.ShapeDtypeStruct((n, 2 * w), BF16),
            jax.ShapeDtypeStruct((n, 2 * w), BF16),
        ],
        compiler_params=_params(("arbitrary",)),
        name="qk_prep",
    )(proj, proj, proj, gq, gk, grp, sf)


def _subln(o, g, out_scale):
    ms = jnp.mean(o * o, axis=-1, keepdims=True)
    return ((o * lax.rsqrt(ms + EPS)) * g) * out_scale


def _attn_kernel(lam_ref, q_ref, k_ref, v_ref, gs_ref, o_ref, m0_ref, a0_ref, m1_ref, a1_ref,
                 *, tq, out_scale):
    i = pl.program_id(2)
    q = q_ref[...]
    qs = (q[:, :LANES], q[:, LANES:])
    states = ((m0_ref, a0_ref), (m1_ref, a1_ref))
    for m_ref, a_ref in states:
        m_ref[...] = jnp.full(m_ref.shape, NEG, F32)
        a_ref[...] = jnp.zeros(a_ref.shape, F32)
    causal = lax.broadcasted_iota(jnp.int32, (tq, tq), 0) >= lax.broadcasted_iota(jnp.int32, (tq, tq), 1)

    def scores(j):
        kt = k_ref[pl.ds(pl.multiple_of(j * tq, tq), tq), :]
        return tuple(lax.dot_general(qs[c], kt[:, c * LANES:(c + 1) * LANES], (((1,), (1,)), ((), ())),
                                     preferred_element_type=F32) for c in range(2))

    def accumulate(j, s_maps, masked):
        vt = v_ref[pl.ds(pl.multiple_of(j * tq, tq), tq), :]
        for s, (m_ref, a_ref) in zip(s_maps, states):
            if masked:
                s = jnp.where(causal, s, NEG)
            m_old = m_ref[...]
            m_new = jnp.maximum(m_old, jnp.max(s, axis=-1, keepdims=True))
            p = jnp.exp2(s - m_new)
            a_ref[...] = jnp.exp2(m_old - m_new) * a_ref[...] + jnp.dot(p.astype(BF16), vt,
                                                                         preferred_element_type=F32)
            m_ref[...] = m_new

    def body(j, s_maps):
        s_next = scores(j + 1)
        accumulate(j, s_maps, False)
        return s_next

    accumulate(i, lax.fori_loop(0, i, body, scores(0)), True)
    a0 = a0_ref[...]
    a1 = a1_ref[...]
    o = a0[:, :LANES] / a0[:, LANES:LANES + 1] - lam_ref[0] * (a1[:, :LANES] / a1[:, LANES:LANES + 1])
    o_ref[...] = _subln(o, gs_ref[...], out_scale).astype(o_ref.dtype)


def _prompt_attention(q2, k2, v2, n_batch, t, lam, subln, out_scale):
    heads = q2.shape[1] // (2 * LANES)
    tq = _row_tile(t, 512)
    nq = t // tq
    kv = pl.BlockSpec((t, 2 * LANES), lambda bi, hi, i, lm: (bi, hi))
    return pl.pallas_call(
        functools.partial(_attn_kernel, tq=tq, out_scale=out_scale),
        grid_spec=pltpu.PrefetchScalarGridSpec(
            num_scalar_prefetch=1,
            grid=(n_batch, heads, nq),
            in_specs=[pl.BlockSpec((tq, 2 * LANES), lambda bi, hi, i, lm: (bi * nq + i, hi)), kv, kv,
                      pl.BlockSpec((1, LANES), lambda bi, hi, i, lm: (0, 0))],
            out_specs=pl.BlockSpec((tq, LANES), lambda bi, hi, i, lm: (bi * nq + i, hi)),
            scratch_shapes=[pltpu.VMEM((tq, 1), F32), pltpu.VMEM((tq, 2 * LANES), F32)] * 2,
        ),
        out_shape=jax.ShapeDtypeStruct((n_batch * t, heads * LANES), BF16),
        compiler_params=_params(("arbitrary", "arbitrary", "arbitrary")),
        name="prompt_diff_attention",
    )(lam.reshape(1), q2, k2, v2, subln.reshape(1, LANES))


QSLOTS = 4
NEW_SLOTS = 8


def _paged_attn_kernel(pt_ref, lam_ref, *refs, pages, page, out_scale):
    kp_refs = refs[:pages]
    vp_refs = refs[pages:2 * pages]
    (q_ref, kn_ref, vn_ref, bias_ref, biasn_ref, slope_ref, gs_ref, o_ref, m_ref, l_ref, a_ref) = refs[2 * pages:]
    g = pl.program_id(1)
    q = q_ref[0]
    slope_c = slope_ref[...]
    nt = (((1,), (1,)), ((), ()))

    def update(tiles, first):
        m_old = jnp.full(m_ref.shape, NEG, F32) if first else m_ref[...]
        m_new = m_old
        for s, shift_c, _ in tiles:
            m_new = jnp.maximum(m_new, jnp.max(s, axis=-1, keepdims=True) + shift_c)
        psum = None
        pv = None
        for s, shift_c, v_tile in tiles:
            p = jnp.exp2(s - (m_new - shift_c))
            ps = jnp.sum(p, axis=-1, keepdims=True)
            d = jnp.dot(p.astype(BF16), v_tile, preferred_element_type=F32)
            psum = ps if psum is None else psum + ps
            pv = d if pv is None else pv + d
        if first:
            l_ref[...] = psum
            a_ref[...] = pv
        else:
            alpha = jnp.exp2(m_old - m_new)
            l_ref[...] = alpha * l_ref[...] + psum
            a_ref[...] = alpha * a_ref[...] + pv
        m_ref[...] = m_new

    @pl.when(g == 0)
    def _():
        s = lax.dot_general(q, kn_ref[0], nt, preferred_element_type=F32) + biasn_ref[...]
        update([(s, jnp.zeros_like(slope_c), vn_ref[0])], True)

    tiles = []
    for p in range(pages):
        kt = kp_refs[p][0].astype(BF16)
        s = lax.dot_general(q, kt, nt, preferred_element_type=F32) + bias_ref[...]
        first_pos = ((g * pages + p) * page).astype(F32)
        tiles.append((s, slope_c * first_pos, vp_refs[p][0].astype(BF16)))
    update(tiles, False)

    @pl.when(g == pl.num_programs(1) - 1)
    def _():
        a = a_ref[...] / l_ref[...]
        half_rows = a.shape[0] // 2
        o = a[:half_rows] - lam_ref[0] * a[half_rows:]
        o_ref[0] = _subln(o, gs_ref[...], out_scale)


def _paged_attention(q2, k2, v2, cache_k, cache_v, page_table, lam, subln, out_scale):
    b, tq, _ = q2.shape
    n_pool, page, heads, _ = cache_k.shape
    half = LANES // 2
    n_pages = page_table.shape[1]
    past_len = n_pages * page
    pages = math.gcd(n_pages, 8)
    rows = 2 * heads * QSLOTS
    assert tq <= QSLOTS and tq <= NEW_SLOTS
    q5 = q2.reshape(b, tq, heads, 2, LANES)
    zero = jnp.zeros((b, tq, heads, half), BF16)
    qmaps = jnp.stack([jnp.concatenate([q5[:, :, :, 0, :half], zero], axis=-1),
                       jnp.concatenate([zero, q5[:, :, :, 1, half:]], axis=-1)], axis=1)
    qrows = jnp.pad(qmaps.transpose(0, 1, 3, 2, 4), ((0, 0), (0, 0), (0, 0), (0, QSLOTS - tq), (0, 0)))
    qrows = qrows.reshape(b, rows, LANES)
    k5 = k2.reshape(b, tq, heads, 2, LANES)
    kn = jnp.concatenate([k5[:, :, :, 0, :half], k5[:, :, :, 1, half:]], axis=-1)
    vn = v2.reshape(b, tq, heads, 2, LANES)[:, :, :, 0, :]
    pad_t = ((0, 0), (0, NEW_SLOTS - tq), (0, 0), (0, 0))
    kn = jnp.pad(kn, pad_t).reshape(b, NEW_SLOTS * heads, LANES)
    vn = jnp.pad(vn, pad_t).reshape(b, NEW_SLOTS * heads, LANES)
    slope2 = _slope_terms(heads).astype(np.float64).sum(axis=-1)
    r = np.arange(rows)
    row_head = (r // QSLOTS) % heads
    row_q = r % QSLOTS

    def bias_matrix(n_tok, key_pos0):
        col = np.arange(n_tok * heads)
        col_tok, col_head = col // heads, col % heads
        dist = (past_len + row_q)[:, None] - (key_pos0 + col_tok)[None, :]
        ok = (row_head[:, None] == col_head[None, :]) & (dist >= 0)
        return jnp.asarray(np.where(ok, -slope2[row_head][:, None] * dist, NEG).astype(np.float32))

    bias_page = bias_matrix(page, 0)
    new_tok = np.arange(NEW_SLOTS * heads) // heads
    bias_new = jnp.where(jnp.asarray(new_tok < tq)[None, :], bias_matrix(NEW_SLOTS, past_len), NEG)
    slope_col = jnp.asarray(slope2[row_head].astype(np.float32).reshape(rows, 1))
    ck = cache_k.reshape(n_pool, page * heads, LANES)
    cv = cache_v.reshape(n_pool, page * heads, LANES)

    def page_spec(p):
        return pl.BlockSpec((1, page * heads, LANES),
                            lambda bi, g, pt, lm: (pt[bi * n_pages + g * pages + p], 0, 0))

    per_b = lambda shape: pl.BlockSpec(shape, lambda bi, g, pt, lm: (bi,) + (0,) * (len(shape) - 1))
    const = lambda shape: pl.BlockSpec(shape, lambda bi, g, pt, lm: (0,) * len(shape))
    out = pl.pallas_call(
        functools.partial(_paged_attn_kernel, pages=pages, page=page, out_scale=out_scale),
        grid_spec=pltpu.PrefetchScalarGridSpec(
            num_scalar_prefetch=2,
            grid=(b, n_pages // pages),
            in_specs=[page_spec(p) for p in range(pages)] * 2 + [
                per_b((1, rows, LANES)), per_b((1, NEW_SLOTS * heads, LANES)), per_b((1, NEW_SLOTS * heads, LANES)),
                const((rows, page * heads)), const((rows, NEW_SLOTS * heads)), const((rows, 1)),
                const((1, LANES))],
            out_specs=per_b((1, rows // 2, LANES)),
            scratch_shapes=[pltpu.VMEM((rows, 1), F32), pltpu.VMEM((rows, 1), F32),
                            pltpu.VMEM((rows, LANES), F32)],
        ),
        out_shape=jax.ShapeDtypeStruct((b, rows // 2, LANES), F32),
        compiler_params=_params(("arbitrary", "arbitrary")),
        name="paged_diff_attention",
    )(page_table.reshape(-1), lam.reshape(1), *([ck] * pages), *([cv] * pages),
      qrows, kn, vn, bias_page, bias_new, slope_col, subln.reshape(1, LANES))
    return out.reshape(b, heads, QSLOTS, LANES)[:, :, :tq].transpose(0, 2, 1, 3).reshape(b, tq, heads * LANES)


def _conv_kernel(x_ref, st_ref, w_ref, b_ref, qo_ref, ko_ref, buf_ref, *, tt, q_scale, carry):
    @pl.when(pl.program_id(1) == 0)
    def _():
        buf_ref[0:SUBLANES, :] = st_ref[0]

    x = x_ref[0]
    buf_ref[SUBLANES:SUBLANES + tt, :] = x
    acc = b_ref[...] + w_ref[CONV_W - 1:CONV_W, :] * x
    for k in range(1, CONV_W):
        acc = acc + w_ref[CONV_W - 1 - k:CONV_W - k, :] * buf_ref[SUBLANES - k:SUBLANES - k + tt, :]
    u = acc * jax.nn.sigmoid(acc)
    c = u.shape[1] // 2
    qo_ref[0] = (u[:, :c] * q_scale).astype(qo_ref.dtype)
    ko_ref[0] = u[:, c:].astype(ko_ref.dtype)
    if carry:
        buf_ref[0:SUBLANES, :] = buf_ref[tt:tt + SUBLANES, :]


def _conv_silu(x, region, n_batch, t, state8, conv_w, conv_b, q_scale):
    c = x.shape[2]
    tt = _row_tile(t, 256)
    nt = t // tt
    w8 = jnp.pad(conv_w, ((0, SUBLANES - CONV_W), (0, 0)))
    out = pl.BlockSpec((1, tt, c // 2), lambda bi, i: (bi, i, 0))
    return pl.pallas_call(
        functools.partial(_conv_kernel, tt=tt, q_scale=q_scale, carry=nt > 1),
        grid=(n_batch, nt),
        in_specs=[
            pl.BlockSpec((1, tt, c), lambda bi, i: (region, bi * nt + i, 0)),
            pl.BlockSpec((1, SUBLANES, c), lambda bi, i: (bi, 0, 0)),
            pl.BlockSpec((SUBLANES, c), lambda bi, i: (0, 0)),
            pl.BlockSpec((1, c), lambda bi, i: (0, 0)),
        ],
        out_specs=[out, out],
        out_shape=[jax.ShapeDtypeStruct((n_batch, t, c // 2), BF16)] * 2,
        scratch_shapes=[pltpu.VMEM((tt + SUBLANES, c), F32)],
        compiler_params=_params(("arbitrary", "arbitrary")),
        name="mlstm_conv_silu",
    )(x, state8, w8, conv_b.reshape(1, c))


def _mlstm_kernel(q_ref, k_ref, v_ref, og_ref, g_ref, bg_ref, on_ref, c0_ref, n0_ref, m0_ref,
                  o_ref, c_ref, n_ref, m_ref, cs_ref, ns_ref, ms_ref, *, heads, dk, dv, rows, chunk, t_valid, bb):
    ci = pl.program_id(1)

    @pl.when(ci == 0)
    def _():
        cs_ref[...] = c0_ref[...]
        ns_ref[...] = n0_ref[...]
        ms_ref[...] = m0_ref[...]

    for u in range(bb):
        _mlstm_chunk(u, q_ref, k_ref, v_ref, og_ref, g_ref, bg_ref, on_ref, o_ref, cs_ref, ns_ref, ms_ref,
                     heads=heads, dk=dk, dv=dv, rows=rows, L=chunk, t_valid=t_valid)

    @pl.when(ci == pl.num_programs(1) - 1)
    def _():
        c_ref[...] = cs_ref[...]
        n_ref[...] = ns_ref[...]
        m_ref[...] = ms_ref[...]


def _mlstm_chunk(u, q_ref, k_ref, v_ref, og_ref, g_ref, bg_ref, on_ref, o_ref, cs_ref, ns_ref, ms_ref,
                 *, heads, dk, dv, rows, L, t_valid):
    def pad_rows(a):
        if rows == L:
            return a
        return jnp.concatenate([a, jnp.zeros((L - rows, a.shape[1]), a.dtype)], axis=0)

    gates = pad_rows(g_ref[u] + bg_ref[...])
    lf_all = jnp.minimum(gates, 0.0) - jnp.log1p(jnp.exp(-jnp.abs(gates)))
    q_all = pad_rows(q_ref[u]).astype(BF16)
    k_all = pad_rows(k_ref[u]).astype(BF16)
    v_all = pad_rows(v_ref[u]).astype(BF16)
    r_i = lax.broadcasted_iota(jnp.int32, (L, L), 0)
    c_i = lax.broadcasted_iota(jnp.int32, (L, L), 1)
    eye = r_i == c_i
    tril = c_i <= r_i
    valid = lax.broadcasted_iota(jnp.int32, (L, 1), 0) < t_valid

    def to_row(col):
        return jnp.sum(jnp.where(eye, col, 0.0), axis=0, keepdims=True)

    H = range(heads)
    per_head = lambda f, *lists: [f(*args) for args in zip(*lists)]
    nt = (((1,), (1,)), ((), ()))
    tn = (((0,), (0,)), ((), ()))
    ms_all = ms_ref[u]
    ns_all = ns_ref[u]
    c_old = [cs_ref[u, h] for h in H]
    n_old = [ns_all[h:h + 1, :] for h in H]
    m_prev = [ms_all[h:h + 1, 0:1] for h in H]
    qh = [q_all[:, h * dk:(h + 1) * dk] for h in H]
    kh = [k_all[:, h * dk:(h + 1) * dk] for h in H]
    vh = [v_all[:, h * dv:(h + 1) * dv] for h in H]
    ig_c = [jnp.where(valid, gates[:, h:h + 1], NEG) for h in H]
    lf_c = [jnp.where(valid, lf_all[:, heads + h:heads + h + 1], 0.0) for h in H]
    ig_r = per_head(to_row, ig_c)
    lf_r = per_head(to_row, lf_c)
    b_c = per_head(lambda r: jnp.sum(jnp.where(tril, r, 0.0), axis=1, keepdims=True), lf_r)
    b_r = per_head(to_row, b_c)
    dmat = per_head(lambda bc, br, ir: jnp.where(tril, (bc - br) + ir, NEG), b_c, b_r, ig_r)
    inter = per_head(lambda bc, mp: bc + mp, b_c, m_prev)
    mt = per_head(lambda it, d: jnp.maximum(it, jnp.max(d, axis=1, keepdims=True)), inter, dmat)
    s = per_head(lambda q, k: lax.dot_general(q, k, nt, preferred_element_type=F32), qh, kh)
    wm = per_head(lambda s_, d, m: s_ * jnp.exp(d - m), s, dmat, mt)
    e_in = per_head(lambda it, m: jnp.exp(it - m), inter, mt)
    qc = per_head(lambda q, c: jnp.dot(q, c.astype(BF16), preferred_element_type=F32), qh, c_old)
    wv = per_head(lambda w, v: jnp.dot(w.astype(BF16), v, preferred_element_type=F32), wm, vh)
    num = per_head(lambda e, a, b: e * a + b, e_in, qc, wv)
    den = per_head(lambda e, q, n, w: e * jnp.sum(q.astype(F32) * n, axis=1, keepdims=True)
                   + jnp.sum(w, axis=1, keepdims=True), e_in, qh, n_old, wm)
    hid = per_head(lambda nu, de, m: nu / jnp.maximum(jnp.abs(de), jnp.exp(-m)), num, den, mt)
    ms_h = per_head(lambda x: jnp.mean(x * x, axis=-1, keepdims=True), hid)
    hn = per_head(lambda x, m: (x * lax.rsqrt(m + EPS)) * on_ref[...], hid, ms_h)
    for h in H:
        gate = jax.nn.sigmoid(og_ref[u, :, h * dv:(h + 1) * dv])
        o_ref[u, :, h * dv:(h + 1) * dv] = (gate * hn[h][:rows]).astype(o_ref.dtype)
    b_last = per_head(lambda r: jnp.sum(r, axis=1, keepdims=True), lf_r)
    g_c = per_head(lambda bl, bc, ic: (bl - bc) + ic, b_last, b_c, ig_c)
    g_r = per_head(lambda bl, br, ir: (bl - br) + ir, b_last, b_r, ig_r)
    m_new = per_head(lambda bl, mp, g: jnp.maximum(bl + mp, jnp.max(g, axis=1, keepdims=True)),
                     b_last, m_prev, g_r)
    e_old = per_head(lambda bl, mp, mn: jnp.exp((bl + mp) - mn), b_last, m_prev, m_new)
    kw = per_head(lambda k, g, mn: k.astype(F32) * jnp.exp(g - mn), kh, g_c, m_new)
    kv = per_head(lambda k, v: lax.dot_general(k.astype(BF16), v, tn, preferred_element_type=F32), kw, vh)
    for h in H:
        cs_ref[u, h] = e_old[h] * c_old[h] + kv[h]
    ns_ref[u] = jnp.concatenate(per_head(lambda e, n, k: e * n + jnp.sum(k, axis=0, keepdims=True),
                                         e_old, n_old, kw), axis=0)
    ms_ref[u] = jnp.concatenate([jnp.broadcast_to(m, (1, LANES)) for m in m_new], axis=0)


def _mlstm(q, k, v_src, v_spec, og_src, og_spec, gates, b_gate, outnorm, c0, n0, m0, t_valid):
    b, t, _ = q.shape
    heads, dk, dv = c0.shape[1:]
    rows = min(t, M_CHUNK)
    nc = t // rows
    bg = jnp.pad(b_gate, (0, LANES - b_gate.shape[0])).reshape(1, LANES)
    m0b = jnp.broadcast_to(m0[:, :, None], (b, heads, LANES))
    chunk = M_CHUNK if rows == M_CHUNK else 2 * SUBLANES
    bb = math.gcd(b, SHORT_SEQS_PER_STEP) if rows < M_CHUNK else 1
    blk = lambda w: pl.BlockSpec((bb, rows, w), lambda bi, ci: (bi, ci, 0))
    per_b = lambda shape: pl.BlockSpec((bb,) + shape, lambda bi, ci: (bi,) + (0,) * len(shape))
    const = lambda shape: pl.BlockSpec(shape, lambda bi, ci: (0,) * len(shape))
    kernel = functools.partial(_mlstm_kernel, heads=heads, dk=dk, dv=dv, rows=rows, chunk=chunk,
                               t_valid=t_valid, bb=bb)
    o, c, n, m = pl.pallas_call(
        kernel,
        grid=(b // bb, nc),
        in_specs=[blk(heads * dk), blk(heads * dk), v_spec(bb, rows, nc), og_spec(bb, rows, nc), blk(LANES),
                  const((1, LANES)), const((1, dv)), per_b((heads, dk, dv)), per_b((heads, dk)),
                  per_b((heads, LANES))],
        out_specs=[blk(heads * dv), per_b((heads, dk, dv)), per_b((heads, dk)), per_b((heads, LANES))],
        out_shape=[jax.ShapeDtypeStruct((b, t, heads * dv), BF16 if rows == M_CHUNK else F32),
                   jax.ShapeDtypeStruct((b, heads, dk, dv), F32),
                   jax.ShapeDtypeStruct((b, heads, dk), F32),
                   jax.ShapeDtypeStruct((b, heads, LANES), F32)],
        scratch_shapes=[pltpu.VMEM((bb, heads, dk, dv), F32), pltpu.VMEM((bb, heads, dk), F32),
                        pltpu.VMEM((bb, heads, LANES), F32)],
        compiler_params=_params(("arbitrary", "arbitrary")),
        name="mlstm_chunkwise",
    )(q, k, v_src, og_src, gates, bg, outnorm.reshape(1, dv), c0, n0, m0b)
    return o, c, n, m[:, :, 0]


def _outproj_kernel(a1_ref, a2_ref, b1_ref, b2_ref, wa_ref, wb_ref, x1_ref, x2_ref, o_ref, *, nb_first):
    def project(a_ref, b_ref, x_ref):
        o_ref[...] = x_ref[...] + (jnp.dot(a_ref[...], wa_ref[...], preferred_element_type=F32)
                                   + jnp.dot(b_ref[...], wb_ref[...], preferred_element_type=F32))

    i = pl.program_id(0)
    pl.when(i < nb_first)(lambda: project(a1_ref, b1_ref, x1_ref))
    pl.when(i >= nb_first)(lambda: project(a2_ref, b2_ref, x2_ref))


def _output_projection(a_groups, b_groups, w, x_groups):
    n1, wa = a_groups[0].shape
    n2 = a_groups[1].shape[0]
    wb = b_groups[0].shape[1]
    d = w.shape[1]
    tm = _row_tile(math.gcd(n1, n2), 512)
    nb_first = n1 // tm
    tn = min(1024, d)
    return pl.pallas_call(
        functools.partial(_outproj_kernel, nb_first=nb_first),
        grid=((n1 + n2) // tm, d // tn),
        in_specs=[
            *_two_group_specs(tm, wa, nb_first),
            *_two_group_specs(tm, wb, nb_first),
            pl.BlockSpec((wa, tn), lambda i, j: (0, j)),
            pl.BlockSpec((wb, tn), lambda i, j: (wa // wb, j)),
            *_two_group_specs(tm, tn, nb_first, n_col=d // tn),
        ],
        out_specs=pl.BlockSpec((tm, tn), lambda i, j: (i, j)),
        out_shape=jax.ShapeDtypeStruct((n1 + n2, d), F32),
        compiler_params=_params(("arbitrary", "arbitrary")),
        name="output_projection",
    )(*a_groups, *b_groups, w, w, *x_groups)


def _router_kernel(x_ref, g_ref, whi_ref, wlo_ref, b_ref, xf_ref, lg_ref):
    x = x_ref[...]
    ms = jnp.mean(x * x, axis=-1, keepdims=True)
    xn = (x * lax.rsqrt(ms + EPS)) * g_ref[...]
    hi = xn.astype(BF16)
    lo = (xn - hi.astype(F32)).astype(BF16)
    xf_ref[...] = xn
    lg_ref[...] = b_ref[...] + jnp.dot(hi, whi_ref[...], preferred_element_type=F32) \
        + (jnp.dot(hi, wlo_ref[...], preferred_element_type=F32)
           + jnp.dot(lo, whi_ref[...], preferred_element_type=F32))


def _ffn_norm_router(x, g, w_route, b_route):
    n, d = x.shape
    tm = _row_tile(n, 256)
    whi = w_route.astype(BF16)
    wlo = (w_route - whi.astype(F32)).astype(BF16)
    const = lambda shape: pl.BlockSpec(shape, lambda i: (0, 0))
    return pl.pallas_call(
        _router_kernel,
        grid=(n // tm,),
        in_specs=[pl.BlockSpec((tm, d), lambda i: (i, 0)), const((1, d)), const((d, LANES)),
                  const((d, LANES)), const((1, LANES))],
        out_specs=[pl.BlockSpec((tm, d), lambda i: (i, 0)), pl.BlockSpec((tm, LANES), lambda i: (i, 0))],
        out_shape=[jax.ShapeDtypeStruct((n, d), F32), jax.ShapeDtypeStruct((n, LANES), F32)],
        compiler_params=_params(("arbitrary",)),
        name="ffn_norm_router",
    )(x, g.reshape(1, d), whi, wlo, b_route.reshape(1, LANES))


MOE_ROWS = 256
MOE_FF_TILE = 512
MOE_OUT_TILE = 2048


def _row_copy(src_ref, src_row, dst_ref, dst_row, sem):
    return pltpu.make_async_copy(src_ref.at[pl.ds(src_row, 1)], dst_ref.at[pl.ds(dst_row, 1)], sem)


DMA_UNROLL = 8


def _gather_rows_kernel(nu_ref, ids_ref, src_ref, o_ref, buf_ref, sem, *, rows):
    def start(r, carry):
        _row_copy(src_ref, ids_ref[0, 0, r], buf_ref, r, sem).start()
        return carry

    def wait(r, carry):
        _row_copy(src_ref, 0, buf_ref, r, sem).wait()
        return carry

    @pl.when(pl.program_id(0) < nu_ref[0])
    def _():
        lax.fori_loop(0, rows, start, 0, unroll=DMA_UNROLL)
        lax.fori_loop(0, rows, wait, 0, unroll=DMA_UNROLL)
        o_ref[...] = buf_ref[...].astype(o_ref.dtype)

    @pl.when(pl.program_id(0) >= nu_ref[0])
    def _():
        o_ref[...] = jnp.zeros(o_ref.shape, o_ref.dtype)


def _gather_rows(src, ids, n_used, out_dtype):
    n_rows = ids.shape[0]
    d = src.shape[1]
    nblk = n_rows // MOE_ROWS
    return pl.pallas_call(
        functools.partial(_gather_rows_kernel, rows=MOE_ROWS),
        grid_spec=pltpu.PrefetchScalarGridSpec(
            num_scalar_prefetch=1,
            grid=(nblk,),
            in_specs=[pl.BlockSpec((1, 1, MOE_ROWS), lambda i, nu: (i, 0, 0), memory_space=pltpu.SMEM),
                      pl.BlockSpec(memory_space=pl.ANY)],
            out_specs=pl.BlockSpec((MOE_ROWS, d), lambda i, nu: (i, 0)),
            scratch_shapes=[pltpu.VMEM((MOE_ROWS, d), src.dtype), pltpu.SemaphoreType.DMA(())],
        ),
        out_shape=jax.ShapeDtypeStruct((n_rows, d), out_dtype),
        compiler_params=_params(("arbitrary",)),
        name="moe_gather_rows",
    )(n_used, ids.reshape(nblk, 1, MOE_ROWS), src)


COMBINE_ROWS = 128


def _combine_kernel(slots_ref, x_ref, yb_ref, oa_ref, ob_ref, buf_ref, sem, *, tc, nb_first):
    n_copies = TOP_K * tc

    def start(r, carry):
        _row_copy(yb_ref, slots_ref[0, 0, r], buf_ref, r, sem).start()
        return carry

    def wait(r, carry):
        _row_copy(yb_ref, 0, buf_ref, r, sem).wait()
        return carry

    lax.fori_loop(0, n_copies, start, 0, unroll=DMA_UNROLL)
    lax.fori_loop(0, n_copies, wait, 0, unroll=DMA_UNROLL)
    acc = buf_ref[0:tc, :]
    for k in range(1, TOP_K):
        acc = acc + buf_ref[k * tc:(k + 1) * tc, :]
    i = pl.program_id(0)

    @pl.when(i < nb_first)
    def _():
        oa_ref[...] = x_ref[...] + acc

    @pl.when(i >= nb_first)
    def _():
        ob_ref[...] = x_ref[...] + acc


def _combine(x, yb, slot2, n_first):
    n, d = x.shape
    tc = _row_tile(math.gcd(n_first, n - n_first), COMBINE_ROWS)
    nblk = n // tc
    nb_first = n_first // tc
    slots = slot2.reshape(nblk, tc, TOP_K).transpose(0, 2, 1).reshape(nblk, 1, TOP_K * tc)
    return pl.pallas_call(
        functools.partial(_combine_kernel, tc=tc, nb_first=nb_first),
        grid=(nblk,),
        in_specs=[pl.BlockSpec((1, 1, TOP_K * tc), lambda i: (i, 0, 0), memory_space=pltpu.SMEM),
                  pl.BlockSpec((tc, d), lambda i: (i, 0)),
                  pl.BlockSpec(memory_space=pl.ANY)],
        out_specs=[pl.BlockSpec((tc, d), lambda i: (jnp.minimum(i, nb_first - 1), 0)),
                   pl.BlockSpec((tc, d), lambda i: (jnp.maximum(i - nb_first, 0), 0))],
        out_shape=[jax.ShapeDtypeStruct((n_first, d), F32), jax.ShapeDtypeStruct((n - n_first, d), F32)],
        scratch_shapes=[pltpu.VMEM((TOP_K * tc, d), F32), pltpu.SemaphoreType.DMA(())],
        compiler_params=_params(("arbitrary",)),
        name="moe_combine",
    )(slots, x, yb)


def _weights_changed(t, ib_ref, is_ref, be_ref):
    prev = jnp.maximum(t - 1, 0)
    return (t == 0) | (be_ref[ib_ref[t]] != be_ref[ib_ref[prev]]) | (is_ref[t] != is_ref[prev])


def _moe_up_kernel(ib_ref, is_ref, be_ref, ni_ref, x_ref, wg_ref, wu_ref, h_ref, wgb_ref, wub_ref):
    t = pl.program_id(0)

    @pl.when(t >= ni_ref[0])
    def _():
        h_ref[...] = jnp.zeros(h_ref.shape, h_ref.dtype)

    @pl.when(t < ni_ref[0])
    def _():
        @pl.when(_weights_changed(t, ib_ref, is_ref, be_ref))
        def _():
            wgb_ref[...] = wg_ref[0].astype(BF16)
            wub_ref[...] = wu_ref[0].astype(BF16)

        x = x_ref[...]
        gt = jnp.dot(x, wgb_ref[...], preferred_element_type=F32)
        up = jnp.dot(x, wub_ref[...], preferred_element_type=F32)
        h_ref[...] = ((gt * jax.nn.sigmoid(gt)) * up).astype(h_ref.dtype)


def _moe_down_kernel(ib_ref, is_ref, be_ref, ni_ref, h_ref, wd_ref, rw_ref, o_ref, wdb_ref):
    t = pl.program_id(0)

    @pl.when(t >= ni_ref[0])
    def _():
        o_ref[...] = jnp.zeros(o_ref.shape, F32)

    @pl.when(t < ni_ref[0])
    def _():
        @pl.when(_weights_changed(t, ib_ref, is_ref, be_ref))
        def _():
            wdb_ref[...] = wd_ref[0].astype(BF16)

        o_ref[...] = jnp.dot(h_ref[...], wdb_ref[...], preferred_element_type=F32) * rw_ref[...]


def _item_tables(block_e, blocks, bstart, n_used, nblk, nsplit):
    b = jnp.arange(nblk, dtype=jnp.int32)
    s = jnp.arange(nsplit, dtype=jnp.int32)
    e = block_e
    pos_live = nsplit * bstart[e][:, None] + s[None, :] * blocks[e][:, None] + (b - bstart[e])[:, None]
    pos_idle = b[:, None] * nsplit + s[None, :]
    pos = jnp.where((b < n_used[0])[:, None], pos_live, pos_idle).reshape(-1)
    item_b = jnp.zeros((nblk * nsplit,), jnp.int32).at[pos].set(jnp.repeat(b, nsplit))
    item_s = jnp.zeros((nblk * nsplit,), jnp.int32).at[pos].set(jnp.tile(s, nblk))
    return item_b, item_s, (n_used * nsplit).astype(jnp.int32)


def _moe_ffn(xb, row_w, block_e, blocks, bstart, n_used, w_gate, w_up, w_down):
    rows, d = xb.shape
    ff = w_gate.shape[2]
    tf = min(MOE_FF_TILE, ff)
    tn = min(MOE_OUT_TILE, d)
    nblk = rows // MOE_ROWS

    def live(t, ni):
        return jnp.minimum(t, ni[0] - 1)

    def run(kernel, name, nsplit, in_specs, out_spec, out_shape, scratch, args):
        item_b, item_s, n_items = _item_tables(block_e, blocks, bstart, n_used, nblk, nsplit)
        return pl.pallas_call(
            kernel,
            grid_spec=pltpu.PrefetchScalarGridSpec(
                num_scalar_prefetch=4, grid=(nblk * nsplit,), in_specs=in_specs, out_specs=out_spec,
                scratch_shapes=scratch),
            out_shape=out_shape,
            compiler_params=_params(("arbitrary",)),
            name=name,
        )(item_b, item_s, block_e, n_items, *args)

    rows_of = lambda w: pl.BlockSpec((MOE_ROWS, w), lambda t, ib, s, be, ni: (ib[live(t, ni)], 0))
    w_cols = lambda k, w: pl.BlockSpec(
        (1, k, w), lambda t, ib, s, be, ni: (be[ib[live(t, ni)]], 0, s[live(t, ni)]))
    own = lambda w: pl.BlockSpec((MOE_ROWS, w), lambda t, ib, s, be, ni: (ib[t], s[t]))
    hidden = run(_moe_up_kernel, "moe_expert_up", ff // tf,
                 [rows_of(d), w_cols(d, tf), w_cols(d, tf)], own(tf),
                 jax.ShapeDtypeStruct((rows, ff), BF16),
                 [pltpu.VMEM((d, tf), BF16), pltpu.VMEM((d, tf), BF16)], (xb, w_gate, w_up))
    return run(_moe_down_kernel, "moe_expert_down", d // tn,
               [rows_of(ff), w_cols(ff, tn), rows_of(1)], own(tn),
               jax.ShapeDtypeStruct((rows, d), F32),
               [pltpu.VMEM((ff, tn), BF16)], (hidden, w_down, row_w))


def _route(logits, n_groups, n_experts):
    per = n_experts // n_groups
    n = logits.shape[0]
    lg = logits[:, :n_groups]
    le = logits[:, n_groups:n_groups + n_experts]
    gsm = jax.nn.softmax(lg, axis=-1)
    gidx = jnp.argmax(gsm, axis=-1)
    gval = jnp.max(gsm, axis=-1)
    le = jnp.take_along_axis(le.reshape(n, n_groups, per), gidx[:, None, None], axis=1)[:, 0]
    assert TOP_K == 2
    i1 = jnp.argmax(le, axis=-1)
    rest = jnp.where(jnp.arange(per)[None, :] == i1[:, None], -jnp.inf, le)
    i2 = jnp.argmax(rest, axis=-1)
    ev = jnp.stack([jnp.max(le, axis=-1), jnp.max(rest, axis=-1)], axis=-1)
    wts = gval[:, None] * jax.nn.softmax(ev, axis=-1)
    eid = (gidx[:, None] * per + jnp.stack([i1, i2], axis=-1)).astype(jnp.int32)
    return eid, wts


def _dispatch(eid, n_experts, n_rows):
    flat = eid.reshape(-1)
    onehot = (flat[:, None] == jnp.arange(n_experts, dtype=jnp.int32)[None, :]).astype(jnp.int32)
    rank = jnp.take_along_axis(jnp.cumsum(onehot, axis=0) - onehot, flat[:, None], axis=1)[:, 0]
    counts = jnp.sum(onehot, axis=0)
    blocks = (counts + MOE_ROWS - 1) // MOE_ROWS
    bend = jnp.cumsum(blocks)
    bstart = bend - blocks
    slot = bstart[flat] * MOE_ROWS + rank
    n_used = bend[-1:].astype(jnp.int32)
    nblk = n_rows // MOE_ROWS
    block_e = jnp.searchsorted(bend, jnp.arange(nblk, dtype=jnp.int32), side='right')
    block_e = jnp.clip(block_e, 0, n_experts - 1).astype(jnp.int32)
    return slot.astype(jnp.int32), block_e, blocks.astype(jnp.int32), bstart.astype(jnp.int32), n_used


def _moe(x1, n_first, norm_ffn, w_group, b_group, w_router, b_router, w_gate, w_up, w_down):
    n, d = x1.shape
    n_groups = w_group.shape[1]
    n_experts = w_router.shape[1]
    used = n_groups + n_experts
    w_route = jnp.pad(jnp.concatenate([w_group, w_router], axis=1), ((0, 0), (0, LANES - used)))
    b_route = jnp.pad(jnp.concatenate([b_group, b_router]), (0, LANES - used))
    xf, logits = _ffn_norm_router(x1, norm_ffn, w_route, b_route)
    eid, wts = _route(logits, n_groups, n_experts)
    n_assign = n * TOP_K
    n_rows = (n_assign // MOE_ROWS + n_experts) * MOE_ROWS
    slot, block_e, blocks, bstart, n_used = _dispatch(eid, n_experts, n_rows)
    row_a = jnp.full((n_rows,), -1, jnp.int32).at[slot].set(jnp.arange(n_assign, dtype=jnp.int32))
    row_tok = jnp.where(row_a >= 0, row_a // TOP_K, jnp.arange(n_rows, dtype=jnp.int32) % n)
    row_w = jnp.where(row_a >= 0, wts.reshape(-1)[jnp.maximum(row_a, 0)], 0.0)
    xb = _gather_rows(xf, row_tok, n_used, BF16)
    yb = _moe_ffn(xb, row_w.reshape(n_rows, 1), block_e, blocks, bstart, n_used, w_gate, w_up, w_down)
    return _combine(x1, yb, slot.reshape(n, TOP_K), n_first)


def _layer(xp, xs, conv_s, c_s, n_s, m_s, cache_k, cache_v, page_table, lambda_init, p):
    (norm_mix, w_in, b_gate, conv_w, conv_b, q_norm, k_norm, da_lambda, da_subln,
     m_outnorm, w_out, norm_ffn, w_group, b_group, w_router, b_router, w_gate, w_up, w_down) = p
    bp, tp, d = xp.shape
    bs, ts, _ = xs.shape
    n_p, n_s_rows = bp * tp, bs * ts
    heads_m, dk, dv = c_s.shape[1:]
    da_heads = cache_k.shape[2]
    da_w = da_heads * LANES
    m_w = heads_m * dv
    conv_c = conv_w.shape[1]
    assert da_w == m_w == conv_c, "the six projection regions are assumed equally wide"
    region = da_w
    n_reg = 6
    lp = da_lambda.astype(F32)
    lam = jnp.exp(jnp.sum(lp[0] * lp[1])) - jnp.exp(jnp.sum(lp[2] * lp[3])) + lambda_init
    out_scale = 1.0 - lambda_init

    xp2, xs2 = xp.reshape(n_p, d), xs.reshape(n_s_rows, d)
    w_g = jnp.pad(w_in[:, n_reg * region:], ((0, 0), (0, LANES - 2 * heads_m))).astype(BF16)
    proj, gates = _input_projection(xp2, xs2, norm_mix, w_in.astype(BF16), w_g, region, n_reg)
    proj_s = proj[:, n_p:]

    k_f32, q2, k2, v2 = _qk_prep(proj, q_norm, k_norm, tp)
    o_da_p = _prompt_attention(q2, k2, v2, bp, tp, lam, da_subln, out_scale)
    sample3 = lambda a: a[n_p:].reshape(bs, ts, 2 * da_w)
    o_da_s = _paged_attention(sample3(q2), sample3(k2), sample3(v2), cache_k, cache_v, page_table,
                              lam, da_subln, out_scale)

    q_scale = dk ** -0.5
    zeros_state = jnp.zeros((bp, SUBLANES, conv_c), F32)
    mq_p, mk_p = _conv_silu(proj, 3, bp, tp, zeros_state, conv_w, conv_b, q_scale)
    assert ts <= SUBLANES
    qk_s = proj_s[3].reshape(bs, ts, conv_c)
    xs16 = jnp.concatenate([jnp.zeros((bs, 2 * SUBLANES - ts - (CONV_W - 1), conv_c), F32), conv_s, qk_s], axis=1)
    xs16 = xs16.reshape(1, bs * 2 * SUBLANES, conv_c)
    mq_s, mk_s = _conv_silu(xs16, 0, bs, 2 * SUBLANES, jnp.zeros((bs, SUBLANES, conv_c), F32),
                            conv_w, conv_b, q_scale)
    pad8 = lambda a: jnp.pad(a, ((0, 0), (0, SUBLANES - ts), (0, 0)))
    mq_s = pad8(mq_s[:, 2 * SUBLANES - ts:]).astype(F32)
    mk_s = pad8(mk_s[:, 2 * SUBLANES - ts:]).astype(F32)

    def region_spec(r, nb_rows):
        def spec(bb, rows, nc):
            assert bb == 1
            return pl.BlockSpec((1, rows, region), lambda bi, ci: (r, bi * nc + ci, 0))
        return spec

    zc = jnp.zeros((bp, heads_m, dk, dv), F32)
    o_m_p, c_p, n_pp, m_p = _mlstm(mq_p, mk_p, proj, region_spec(4, n_p), proj, region_spec(5, n_p),
                                   gates[:n_p].reshape(bp, tp, LANES), b_gate, m_outnorm,
                                   zc, jnp.zeros((bp, heads_m, dk), F32), jnp.zeros((bp, heads_m), F32), M_CHUNK)
    seq_spec = lambda bb, rows, nc: pl.BlockSpec((bb, rows, region), lambda bi, ci: (bi, ci, 0))
    v_s8 = pad8(proj_s[4].reshape(bs, ts, region))
    og_s8 = pad8(proj_s[5].reshape(bs, ts, region))
    g_s8 = pad8(gates[n_p:].reshape(bs, ts, LANES))
    o_m_s, c_new, n_new, m_new = _mlstm(mq_s, mk_s, v_s8, seq_spec, og_s8, seq_spec, g_s8, b_gate, m_outnorm,
                                        c_s, n_s, m_s, ts)

    o_da = (o_da_p, o_da_s.reshape(n_s_rows, da_w).astype(BF16))
    o_m = (o_m_p.reshape(n_p, m_w), o_m_s[:, :ts].reshape(n_s_rows, m_w).astype(BF16))
    x1 = _output_projection(o_da, o_m, w_out.astype(BF16), (xp2, xs2))
    y_p, y_s = _moe(x1, n_p, norm_ffn, w_group, b_group, w_router, b_router, w_gate, w_up, w_down)

    k4 = lambda a, b_, t_: a.reshape(b_, t_, da_heads, LANES)
    conv_tail = lambda a, b_, t_: jnp.stack([a[(bi + 1) * t_ - (CONV_W - 1):(bi + 1) * t_] for bi in range(b_)])
    conv_new_s = jnp.concatenate([conv_s, qk_s], axis=1)[:, ts:]
    outs_p = (k4(k_f32[:n_p], bp, tp), k4(proj[2, :n_p], bp, tp), c_p, n_pp, m_p, conv_tail(proj[3], bp, tp))
    outs_s = (k4(k_f32[n_p:], bs, ts), k4(proj_s[2], bs, ts), c_new, n_new, m_new, conv_new_s)
    return y_p.reshape(bp, tp, d), y_s.reshape(bs, ts, d), outs_p, outs_s


def kernel(x_prompt, x_sample, cache_k, cache_v, state_C, state_n, state_m, state_conv, page_table, norm_mix, w_in, b_gate, conv_w, conv_b, q_norm, k_norm, da_lambda, da_subln, m_outnorm, w_out, norm_ffn, w_group, b_group, w_router, b_router, w_gate, w_up, w_down):
    depth = w_in.shape[0]
    yp, ys = x_prompt, x_sample
    outs = [[] for _ in range(12)]
    for l in range(depth):
        lambda_init = 0.8 - 0.6 * math.exp(-0.3 * l)
        p = (norm_mix[l], w_in[l], b_gate[l], conv_w[l], conv_b[l], q_norm[l], k_norm[l], da_lambda[l],
             da_subln[l], m_outnorm[l], w_out[l], norm_ffn[l], w_group[l], b_group[l], w_router[l],
             b_router[l], w_gate[l], w_up[l], w_down[l])
        yp, ys, outs_p, outs_s = _layer(yp, ys, state_conv[l], state_C[l], state_n[l], state_m[l],
                                        cache_k[l], cache_v[l], page_table, lambda_init, p)
        for i, a in enumerate(outs_p + outs_s):
            outs[i].append(a)
    return (yp, ys) + tuple(jnp.stack(o) for o in outs)
```

```python
import functools
import math

import numpy as np
import jax
import jax.numpy as jnp
from jax import lax
from jax.experimental import pallas as pl
from jax.experimental.pallas import tpu as pltpu

F32 = jnp.float32
BF16 = jnp.bfloat16

EPS = 1e-6
ALIBI_MAX = 8.0
CONV_W = 4
TOP_K = 2
M_CHUNK = 128
SHORT_SEQS_PER_STEP = 4
NEG = -1e30

LANES = 128
SUBLANES = 8
VMEM_BYTES_V7X = 64 * 1024 * 1024
VMEM_LIMIT = VMEM_BYTES_V7X - 8 * 1024 * 1024


def _row_tile(n, cap):
    t = cap
    while t > SUBLANES and n % t:
        t //= 2
    assert n % t == 0, (n, t)
    return t


def _params(sem, vmem=None):
    return pltpu.CompilerParams(dimension_semantics=sem, vmem_limit_bytes=vmem or VMEM_LIMIT)


def _two_group_specs(tm, width, nb_first, n_col=1, **spec_kwargs):
    first = pl.BlockSpec((tm, width), lambda i, j: (jnp.minimum(i, nb_first - 1),
                                                    jnp.where(i < nb_first, j, n_col - 1) if n_col > 1 else 0),
                         **spec_kwargs)
    second = pl.BlockSpec((tm, width), lambda i, j: (jnp.maximum(i - nb_first, 0),
                                                     jnp.where(i >= nb_first, j, 0) if n_col > 1 else 0),
                          **spec_kwargs)
    return first, second


def _proj_kernel(xa_ref, xb_ref, g_ref, w_ref, wg_ref, o_ref, og_ref, xn_ref, *, nb_first):
    def normalise(x_ref):
        x = x_ref[...]
        ms = jnp.mean(x * x, axis=-1, keepdims=True)
        xn = ((x * lax.rsqrt(ms + EPS)) * g_ref[...]).astype(BF16)
        xn_ref[...] = xn
        og_ref[...] = jnp.dot(xn, wg_ref[...], preferred_element_type=F32)

    i = pl.program_id(0)
    first_col = pl.program_id(1) == 0
    pl.when(first_col & (i < nb_first))(lambda: normalise(xa_ref))
    pl.when(first_col & (i >= nb_first))(lambda: normalise(xb_ref))
    o_ref[0] = jnp.dot(xn_ref[...], w_ref[...], preferred_element_type=F32)


def _input_projection(xa, xb, g, w, wg, region, n_reg):
    d = xa.shape[1]
    n = xa.shape[0] + xb.shape[0]
    e = n_reg * region
    tm = _row_tile(math.gcd(xa.shape[0], xb.shape[0]), 512)
    nb_first = xa.shape[0] // tm
    tn = min(1024, region)
    per = region // tn
    return pl.pallas_call(
        functools.partial(_proj_kernel, nb_first=nb_first),
        grid=(n // tm, e // tn),
        in_specs=[
            *_two_group_specs(tm, d, nb_first, pipeline_mode=pl.Buffered(1)),
            pl.BlockSpec((1, d), lambda i, j: (0, 0)),
            pl.BlockSpec((d, tn), lambda i, j: (0, j)),
            pl.BlockSpec((d, LANES), lambda i, j: (0, 0)),
        ],
        out_specs=[
            pl.BlockSpec((1, tm, tn), lambda i, j: (j // per, i, j % per)),
            pl.BlockSpec((tm, LANES), lambda i, j: (i, 0)),
        ],
        out_shape=[
            jax.ShapeDtypeStruct((n_reg, n, region), F32),
            jax.ShapeDtypeStruct((n, LANES), F32),
        ],
        scratch_shapes=[pltpu.VMEM((tm, d), BF16)],
        compiler_params=_params(("arbitrary", "arbitrary")),
        name="input_projection",
    )(xa, xb, g.reshape(1, d), w, wg)


SLOPE_TERMS = 4
FEAT = 2 * SLOPE_TERMS
POS_SPLIT = 64
LOG2E = math.log2(math.e)


def _alibi_slopes(n):
    return np.array([2.0 ** (-ALIBI_MAX * (h + 1) / n) for h in range(n)], np.float32)


def _slope_terms(heads):
    import ml_dtypes
    rest = _alibi_slopes(heads).astype(np.float64) * LOG2E
    terms = []
    for _ in range(SLOPE_TERMS):
        t = rest.astype(ml_dtypes.bfloat16).astype(np.float64)
        terms.append(t)
        rest = rest - t
    return np.stack(terms, axis=-1).astype(np.float32)


def _qkprep_kernel(q_ref, k_ref, v_ref, gq_ref, gk_ref, grp_ref, sf_ref, ko_ref, q2_ref, k2_ref, v2_ref,
                   *, heads, half, q_scale, tm, t_seq):
    grp = grp_ref[...]
    lane = lax.broadcasted_iota(jnp.int32, (1, LANES), 1)
    lo_half = lane < half
    fl = lane & (half - 1)
    pos = (pl.program_id(0) * tm) % t_seq + lax.broadcasted_iota(jnp.int32, (tm, 1), 0)
    pos_hi = (pos - (pos & (POS_SPLIT - 1))).astype(F32)
    pos_lo = (pos & (POS_SPLIT - 1)).astype(F32)
    kfeat = jnp.where(fl < FEAT, jnp.where((fl & 1) == 0, pos_hi, pos_lo), 0.0)
    ones_col = jnp.broadcast_to(jnp.where(lane == 0, 1.0, 0.0), (tm, LANES)).astype(BF16)

    def half_norm(x, g):
        sq = x * x
        hi = sq.astype(BF16)
        lo = (sq - hi.astype(F32)).astype(BF16)
        s = jnp.dot(hi, grp, preferred_element_type=F32) + jnp.dot(lo, grp, preferred_element_type=F32)
        return (x * lax.rsqrt(s * (1.0 / half) + EPS)) * g

    for h in range(heads):
        sl = slice(h * LANES, (h + 1) * LANES)
        m0 = slice(2 * h * LANES, (2 * h + 1) * LANES)
        m1 = slice((2 * h + 1) * LANES, (2 * h + 2) * LANES)
        qn = half_norm(q_ref[0, :, sl], gq_ref[...]) * q_scale
        sf = sf_ref[h:h + 1, :]
        q2_ref[:, m0] = jnp.where(lo_half, qn, sf).astype(BF16)
        q2_ref[:, m1] = jnp.where(lo_half, sf, qn).astype(BF16)
        kn = half_norm(k_ref[0, :, sl], gk_ref[...])
        ko_ref[:, sl] = kn
        k2_ref[:, m0] = jnp.where(lo_half, kn, kfeat).astype(BF16)
        k2_ref[:, m1] = jnp.where(lo_half, kfeat, kn).astype(BF16)
        v2_ref[:, m0] = v_ref[0, :, sl].astype(BF16)
        v2_ref[:, m1] = ones_col


def _qk_prep(proj, q_norm, k_norm, t_seq):
    _, n, w = proj.shape
    heads = w // LANES
    half = q_norm.shape[0]
    assert 2 * half == LANES and FEAT <= half
    tm = _row_tile(math.gcd(n, t_seq), 256)
    assert t_seq <= POS_SPLIT * 256, "key positions must split into two bf16-exact factors"
    lane = np.arange(LANES)
    grp = jnp.asarray((lane[:, None] // half == lane[None, :] // half).astype(np.float32), BF16)
    gq = jnp.concatenate([q_norm, q_norm]).reshape(1, LANES)
    gk = jnp.concatenate([k_norm, k_norm]).reshape(1, LANES)
    terms = _slope_terms(heads)
    fl = lane % half
    sf = jnp.asarray(np.where(fl[None, :] < FEAT, terms[:, np.minimum(fl // 2, SLOPE_TERMS - 1)], 0.0))
    row = lambda r: pl.BlockSpec((1, tm, w), lambda i: (r, i, 0))
    full = lambda shape: pl.BlockSpec(shape, lambda i: (0, 0))
    out1 = pl.BlockSpec((tm, w), lambda i: (i, 0))
    out2 = pl.BlockSpec((tm, 2 * w), lambda i: (i, 0))
    return pl.pallas_call(
        functools.partial(_qkprep_kernel, heads=heads, half=half, q_scale=half ** -0.5 * LOG2E,
                          tm=tm, t_seq=t_seq),
        grid=(n // tm,),
        in_specs=[row(0), row(1), row(2), full((1, LANES)), full((1, LANES)), full((LANES, LANES)),
                  full((heads, LANES))],
        out_specs=[out1, out2, out2, out2],
        out_shape=[
            jax.ShapeDtypeStruct((n, w), F32),
            jax.ShapeDtypeStruct((n, 2 * w), BF16),
            jax.ShapeDtypeStruct((n, 2 * w), BF16),
            jax.ShapeDtypeStruct((n, 2 * w), BF16),
        ],
        compiler_params=_params(("arbitrary",)),
        name="qk_prep",
    )(proj, proj, proj, gq, gk, grp, sf)


def _subln(o, g, out_scale):
    ms = jnp.mean(o * o, axis=-1, keepdims=True)
    return ((o * lax.rsqrt(ms + EPS)) * g) * out_scale


def _attn_kernel(lam_ref, q_ref, k_ref, v_ref, gs_ref, o_ref, m0_ref, a0_ref, m1_ref, a1_ref,
                 *, tq, out_scale):
    i = pl.program_id(2)
    q = q_ref[...]
    qs = (q[:, :LANES], q[:, LANES:])
    states = ((m0_ref, a0_ref), (m1_ref, a1_ref))
    for m_ref, a_ref in states:
        m_ref[...] = jnp.full(m_ref.shape, NEG, F32)
        a_ref[...] = jnp.zeros(a_ref.shape, F32)
    causal = lax.broadcasted_iota(jnp.int32, (tq, tq), 0) >= lax.broadcasted_iota(jnp.int32, (tq, tq), 1)

    def scores(j):
        kt = k_ref[pl.ds(pl.multiple_of(j * tq, tq), tq), :]
        return tuple(lax.dot_general(qs[c], kt[:, c * LANES:(c + 1) * LANES], (((1,), (1,)), ((), ())),
                                     preferred_element_type=F32) for c in range(2))

    def accumulate(j, s_maps, masked):
        vt = v_ref[pl.ds(pl.multiple_of(j * tq, tq), tq), :]
        for s, (m_ref, a_ref) in zip(s_maps, states):
            if masked:
                s = jnp.where(causal, s, NEG)
            m_old = m_ref[...]
            m_new = jnp.maximum(m_old, jnp.max(s, axis=-1, keepdims=True))
            p = jnp.exp2(s - m_new)
            a_ref[...] = jnp.exp2(m_old - m_new) * a_ref[...] + jnp.dot(p.astype(BF16), vt,
                                                                         preferred_element_type=F32)
            m_ref[...] = m_new

    def body(j, s_maps):
        s_next = scores(j + 1)
        accumulate(j, s_maps, False)
        return s_next

    accumulate(i, lax.fori_loop(0, i, body, scores(0)), True)
    a0 = a0_ref[...]
    a1 = a1_ref[...]
    o = a0[:, :LANES] / a0[:, LANES:LANES + 1] - lam_ref[0] * (a1[:, :LANES] / a1[:, LANES:LANES + 1])
    o_ref[...] = _subln(o, gs_ref[...], out_scale).astype(o_ref.dtype)


def _prompt_attention(q2, k2, v2, n_batch, t, lam, subln, out_scale):
    heads = q2.shape[1] // (2 * LANES)
    tq = _row_tile(t, 512)
    nq = t // tq
    kv = pl.BlockSpec((t, 2 * LANES), lambda bi, hi, i, lm: (bi, hi))
    return pl.pallas_call(
        functools.partial(_attn_kernel, tq=tq, out_scale=out_scale),
        grid_spec=pltpu.PrefetchScalarGridSpec(
            num_scalar_prefetch=1,
            grid=(n_batch, heads, nq),
            in_specs=[pl.BlockSpec((tq, 2 * LANES), lambda bi, hi, i, lm: (bi * nq + i, hi)), kv, kv,
                      pl.BlockSpec((1, LANES), lambda bi, hi, i, lm: (0, 0))],
            out_specs=pl.BlockSpec((tq, LANES), lambda bi, hi, i, lm: (bi * nq + i, hi)),
            scratch_shapes=[pltpu.VMEM((tq, 1), F32), pltpu.VMEM((tq, 2 * LANES), F32)] * 2,
        ),
        out_shape=jax.ShapeDtypeStruct((n_batch * t, heads * LANES), BF16),
        compiler_params=_params(("arbitrary", "arbitrary", "arbitrary")),
        name="prompt_diff_attention",
    )(lam.reshape(1), q2, k2, v2, subln.reshape(1, LANES))


QSLOTS = 4
NEW_SLOTS = 8


def _paged_attn_kernel(pt_ref, lam_ref, *refs, pages, page, out_scale):
    kp_refs = refs[:pages]
    vp_refs = refs[pages:2 * pages]
    (q_ref, kn_ref, vn_ref, bias_ref, biasn_ref, slope_ref, gs_ref, o_ref, m_ref, l_ref, a_ref) = refs[2 * pages:]
    g = pl.program_id(1)
    q = q_ref[0]
    slope_c = slope_ref[...]
    nt = (((1,), (1,)), ((), ()))

    def update(tiles, first):
        m_old = jnp.full(m_ref.shape, NEG, F32) if first else m_ref[...]
        m_new = m_old
        for s, shift_c, _ in tiles:
            m_new = jnp.maximum(m_new, jnp.max(s, axis=-1, keepdims=True) + shift_c)
        psum = None
        pv = None
        for s, shift_c, v_tile in tiles:
            p = jnp.exp2(s - (m_new - shift_c))
            ps = jnp.sum(p, axis=-1, keepdims=True)
            d = jnp.dot(p.astype(BF16), v_tile, preferred_element_type=F32)
            psum = ps if psum is None else psum + ps
            pv = d if pv is None else pv + d
        if first:
            l_ref[...] = psum
            a_ref[...] = pv
        else:
            alpha = jnp.exp2(m_old - m_new)
            l_ref[...] = alpha * l_ref[...] + psum
            a_ref[...] = alpha * a_ref[...] + pv
        m_ref[...] = m_new

    @pl.when(g == 0)
    def _():
        s = lax.dot_general(q, kn_ref[0], nt, preferred_element_type=F32) + biasn_ref[...]
        update([(s, jnp.zeros_like(slope_c), vn_ref[0])], True)

    tiles = []
    for p in range(pages):
        kt = kp_refs[p][0].astype(BF16)
        s = lax.dot_general(q, kt, nt, preferred_element_type=F32) + bias_ref[...]
        first_pos = ((g * pages + p) * page).astype(F32)
        tiles.append((s, slope_c * first_pos, vp_refs[p][0].astype(BF16)))
    update(tiles, False)

    @pl.when(g == pl.num_programs(1) - 1)
    def _():
        a = a_ref[...] / l_ref[...]
        half_rows = a.shape[0] // 2
        o = a[:half_rows] - lam_ref[0] * a[half_rows:]
        o_ref[0] = _subln(o, gs_ref[...], out_scale)


def _paged_attention(q2, k2, v2, cache_k, cache_v, page_table, lam, subln, out_scale):
    b, tq, _ = q2.shape
    n_pool, page, heads, _ = cache_k.shape
    half = LANES // 2
    n_pages = page_table.shape[1]
    past_len = n_pages * page
    pages = math.gcd(n_pages, 8)
    rows = 2 * heads * QSLOTS
    assert tq <= QSLOTS and tq <= NEW_SLOTS
    q5 = q2.reshape(b, tq, heads, 2, LANES)
    zero = jnp.zeros((b, tq, heads, half), BF16)
    qmaps = jnp.stack([jnp.concatenate([q5[:, :, :, 0, :half], zero], axis=-1),
                       jnp.concatenate([zero, q5[:, :, :, 1, half:]], axis=-1)], axis=1)
    qrows = jnp.pad(qmaps.transpose(0, 1, 3, 2, 4), ((0, 0), (0, 0), (0, 0), (0, QSLOTS - tq), (0, 0)))
    qrows = qrows.reshape(b, rows, LANES)
    k5 = k2.reshape(b, tq, heads, 2, LANES)
    kn = jnp.concatenate([k5[:, :, :, 0, :half], k5[:, :, :, 1, half:]], axis=-1)
    vn = v2.reshape(b, tq, heads, 2, LANES)[:, :, :, 0, :]
    pad_t = ((0, 0), (0, NEW_SLOTS - tq), (0, 0), (0, 0))
    kn = jnp.pad(kn, pad_t).reshape(b, NEW_SLOTS * heads, LANES)
    vn = jnp.pad(vn, pad_t).reshape(b, NEW_SLOTS * heads, LANES)
    slope2 = _slope_terms(heads).astype(np.float64).sum(axis=-1)
    r = np.arange(rows)
    row_head = (r // QSLOTS) % heads
    row_q = r % QSLOTS

    def bias_matrix(n_tok, key_pos0):
        col = np.arange(n_tok * heads)
        col_tok, col_head = col // heads, col % heads
        dist = (past_len + row_q)[:, None] - (key_pos0 + col_tok)[None, :]
        ok = (row_head[:, None] == col_head[None, :]) & (dist >= 0)
        return jnp.asarray(np.where(ok, -slope2[row_head][:, None] * dist, NEG).astype(np.float32))

    bias_page = bias_matrix(page, 0)
    new_tok = np.arange(NEW_SLOTS * heads) // heads
    bias_new = jnp.where(jnp.asarray(new_tok < tq)[None, :], bias_matrix(NEW_SLOTS, past_len), NEG)
    slope_col = jnp.asarray(slope2[row_head].astype(np.float32).reshape(rows, 1))
    ck = cache_k.reshape(n_pool, page * heads, LANES)
    cv = cache_v.reshape(n_pool, page * heads, LANES)

    def page_spec(p):
        return pl.BlockSpec((1, page * heads, LANES),
                            lambda bi, g, pt, lm: (pt[bi * n_pages + g * pages + p], 0, 0))

    per_b = lambda shape: pl.BlockSpec(shape, lambda bi, g, pt, lm: (bi,) + (0,) * (len(shape) - 1))
    const = lambda shape: pl.BlockSpec(shape, lambda bi, g, pt, lm: (0,) * len(shape))
    out = pl.pallas_call(
        functools.partial(_paged_attn_kernel, pages=pages, page=page, out_scale=out_scale),
        grid_spec=pltpu.PrefetchScalarGridSpec(
            num_scalar_prefetch=2,
            grid=(b, n_pages // pages),
            in_specs=[page_spec(p) for p in range(pages)] * 2 + [
                per_b((1, rows, LANES)), per_b((1, NEW_SLOTS * heads, LANES)), per_b((1, NEW_SLOTS * heads, LANES)),
                const((rows, page * heads)), const((rows, NEW_SLOTS * heads)), const((rows, 1)),
                const((1, LANES))],
            out_specs=per_b((1, rows // 2, LANES)),
            scratch_shapes=[pltpu.VMEM((rows, 1), F32), pltpu.VMEM((rows, 1), F32),
                            pltpu.VMEM((rows, LANES), F32)],
        ),
        out_shape=jax.ShapeDtypeStruct((b, rows // 2, LANES), F32),
        compiler_params=_params(("arbitrary", "arbitrary")),
        name="paged_diff_attention",
    )(page_table.reshape(-1), lam.reshape(1), *([ck] * pages), *([cv] * pages),
      qrows, kn, vn, bias_page, bias_new, slope_col, subln.reshape(1, LANES))
    return out.reshape(b, heads, QSLOTS, LANES)[:, :, :tq].transpose(0, 2, 1, 3).reshape(b, tq, heads * LANES)


def _conv_kernel(x_ref, st_ref, w_ref, b_ref, qo_ref, ko_ref, buf_ref, *, tt, q_scale, carry):
    @pl.when(pl.program_id(1) == 0)
    def _():
        buf_ref[0:SUBLANES, :] = st_ref[0]

    x = x_ref[0]
    buf_ref[SUBLANES:SUBLANES + tt, :] = x
    acc = b_ref[...] + w_ref[CONV_W - 1:CONV_W, :] * x
    for k in range(1, CONV_W):
        acc = acc + w_ref[CONV_W - 1 - k:CONV_W - k, :] * buf_ref[SUBLANES - k:SUBLANES - k + tt, :]
    u = acc * jax.nn.sigmoid(acc)
    c = u.shape[1] // 2
    qo_ref[0] = (u[:, :c] * q_scale).astype(qo_ref.dtype)
    ko_ref[0] = u[:, c:].astype(ko_ref.dtype)
    if carry:
        buf_ref[0:SUBLANES, :] = buf_ref[tt:tt + SUBLANES, :]


def _conv_silu(x, region, n_batch, t, state8, conv_w, conv_b, q_scale):
    c = x.shape[2]
    tt = _row_tile(t, 256)
    nt = t // tt
    w8 = jnp.pad(conv_w, ((0, SUBLANES - CONV_W), (0, 0)))
    out = pl.BlockSpec((1, tt, c // 2), lambda bi, i: (bi, i, 0))
    return pl.pallas_call(
        functools.partial(_conv_kernel, tt=tt, q_scale=q_scale, carry=nt > 1),
        grid=(n_batch, nt),
        in_specs=[
            pl.BlockSpec((1, tt, c), lambda bi, i: (region, bi * nt + i, 0)),
            pl.BlockSpec((1, SUBLANES, c), lambda bi, i: (bi, 0, 0)),
            pl.BlockSpec((SUBLANES, c), lambda bi, i: (0, 0)),
            pl.BlockSpec((1, c), lambda bi, i: (0, 0)),
        ],
        out_specs=[out, out],
        out_shape=[jax.ShapeDtypeStruct((n_batch, t, c // 2), BF16)] * 2,
        scratch_shapes=[pltpu.VMEM((tt + SUBLANES, c), F32)],
        compiler_params=_params(("arbitrary", "arbitrary")),
        name="mlstm_conv_silu",
    )(x, state8, w8, conv_b.reshape(1, c))


def _mlstm_kernel(q_ref, k_ref, v_ref, og_ref, g_ref, bg_ref, on_ref, c0_ref, n0_ref, m0_ref,
                  o_ref, c_ref, n_ref, m_ref, cs_ref, ns_ref, ms_ref, *, heads, dk, dv, rows, chunk, t_valid, bb):
    ci = pl.program_id(1)

    @pl.when(ci == 0)
    def _():
        cs_ref[...] = c0_ref[...]
        ns_ref[...] = n0_ref[...]
        ms_ref[...] = m0_ref[...]

    for u in range(bb):
        _mlstm_chunk(u, q_ref, k_ref, v_ref, og_ref, g_ref, bg_ref, on_ref, o_ref, cs_ref, ns_ref, ms_ref,
                     heads=heads, dk=dk, dv=dv, rows=rows, L=chunk, t_valid=t_valid)

    @pl.when(ci == pl.num_programs(1) - 1)
    def _():
        c_ref[...] = cs_ref[...]
        n_ref[...] = ns_ref[...]
        m_ref[...] = ms_ref[...]


def _mlstm_chunk(u, q_ref, k_ref, v_ref, og_ref, g_ref, bg_ref, on_ref, o_ref, cs_ref, ns_ref, ms_ref,
                 *, heads, dk, dv, rows, L, t_valid):
    def pad_rows(a):
        if rows == L:
            return a
        return jnp.concatenate([a, jnp.zeros((L - rows, a.shape[1]), a.dtype)], axis=0)

    gates = pad_rows(g_ref[u] + bg_ref[...])
    lf_all = jnp.minimum(gates, 0.0) - jnp.log1p(jnp.exp(-jnp.abs(gates)))
    q_all = pad_rows(q_ref[u]).astype(BF16)
    k_all = pad_rows(k_ref[u]).astype(BF16)
    v_all = pad_rows(v_ref[u]).astype(BF16)
    r_i = lax.broadcasted_iota(jnp.int32, (L, L), 0)
    c_i = lax.broadcasted_iota(jnp.int32, (L, L), 1)
    eye = r_i == c_i
    tril = c_i <= r_i
    valid = lax.broadcasted_iota(jnp.int32, (L, 1), 0) < t_valid

    def to_row(col):
        return jnp.sum(jnp.where(eye, col, 0.0), axis=0, keepdims=True)

    H = range(heads)
    per_head = lambda f, *lists: [f(*args) for args in zip(*lists)]
    nt = (((1,), (1,)), ((), ()))
    tn = (((0,), (0,)), ((), ()))
    ms_all = ms_ref[u]
    ns_all = ns_ref[u]
    c_old = [cs_ref[u, h] for h in H]
    n_old = [ns_all[h:h + 1, :] for h in H]
    m_prev = [ms_all[h:h + 1, 0:1] for h in H]
    qh = [q_all[:, h * dk:(h + 1) * dk] for h in H]
    kh = [k_all[:, h * dk:(h + 1) * dk] for h in H]
    vh = [v_all[:, h * dv:(h + 1) * dv] for h in H]
    ig_c = [jnp.where(valid, gates[:, h:h + 1], NEG) for h in H]
    lf_c = [jnp.where(valid, lf_all[:, heads + h:heads + h + 1], 0.0) for h in H]
    ig_r = per_head(to_row, ig_c)
    lf_r = per_head(to_row, lf_c)
    b_c = per_head(lambda r: jnp.sum(jnp.where(tril, r, 0.0), axis=1, keepdims=True), lf_r)
    b_r = per_head(to_row, b_c)
    dmat = per_head(lambda bc, br, ir: jnp.where(tril, (bc - br) + ir, NEG), b_c, b_r, ig_r)
    inter = per_head(lambda bc, mp: bc + mp, b_c, m_prev)
    mt = per_head(lambda it, d: jnp.maximum(it, jnp.max(d, axis=1, keepdims=True)), inter, dmat)
    s = per_head(lambda q, k: lax.dot_general(q, k, nt, preferred_element_type=F32), qh, kh)
    wm = per_head(lambda s_, d, m: s_ * jnp.exp(d - m), s, dmat, mt)
    e_in = per_head(lambda it, m: jnp.exp(it - m), inter, mt)
    qc = per_head(lambda q, c: jnp.dot(q, c.astype(BF16), preferred_element_type=F32), qh, c_old)
    wv = per_head(lambda w, v: jnp.dot(w.astype(BF16), v, preferred_element_type=F32), wm, vh)
    num = per_head(lambda e, a, b: e * a + b, e_in, qc, wv)
    den = per_head(lambda e, q, n, w: e * jnp.sum(q.astype(F32) * n, axis=1, keepdims=True)
                   + jnp.sum(w, axis=1, keepdims=True), e_in, qh, n_old, wm)
    hid = per_head(lambda nu, de, m: nu / jnp.maximum(jnp.abs(de), jnp.exp(-m)), num, den, mt)
    ms_h = per_head(lambda x: jnp.mean(x * x, axis=-1, keepdims=True), hid)
    hn = per_head(lambda x, m: (x * lax.rsqrt(m + EPS)) * on_ref[...], hid, ms_h)
    for h in H:
        gate = jax.nn.sigmoid(og_ref[u, :, h * dv:(h + 1) * dv])
        o_ref[u, :, h * dv:(h + 1) * dv] = (gate * hn[h][:rows]).astype(o_ref.dtype)
    b_last = per_head(lambda r: jnp.sum(r, axis=1, keepdims=True), lf_r)
    g_c = per_head(lambda bl, bc, ic: (bl - bc) + ic, b_last, b_c, ig_c)
    g_r = per_head(lambda bl, br, ir: (bl - br) + ir, b_last, b_r, ig_r)
    m_new = per_head(lambda bl, mp, g: jnp.maximum(bl + mp, jnp.max(g, axis=1, keepdims=True)),
                     b_last, m_prev, g_r)
    e_old = per_head(lambda bl, mp, mn: jnp.exp((bl + mp) - mn), b_last, m_prev, m_new)
    kw = per_head(lambda k, g, mn: k.astype(F32) * jnp.exp(g - mn), kh, g_c, m_new)
    kv = per_head(lambda k, v: lax.dot_general(k.astype(BF16), v, tn, preferred_element_type=F32), kw, vh)
    for h in H:
        cs_ref[u, h] = e_old[h] * c_old[h] + kv[h]
    ns_ref[u] = jnp.concatenate(per_head(lambda e, n, k: e * n + jnp.sum(k, axis=0, keepdims=True),
                                         e_old, n_old, kw), axis=0)
    ms_ref[u] = jnp.concatenate([jnp.broadcast_to(m, (1, LANES)) for m in m_new], axis=0)


def _mlstm(q, k, v_src, v_spec, og_src, og_spec, gates, b_gate, outnorm, c0, n0, m0, t_valid):
    b, t, _ = q.shape
    heads, dk, dv = c0.shape[1:]
    rows = min(t, M_CHUNK)
    nc = t // rows
    bg = jnp.pad(b_gate, (0, LANES - b_gate.shape[0])).reshape(1, LANES)
    m0b = jnp.broadcast_to(m0[:, :, None], (b, heads, LANES))
    chunk = M_CHUNK if rows == M_CHUNK else 2 * SUBLANES
    bb = math.gcd(b, SHORT_SEQS_PER_STEP) if rows < M_CHUNK else 1
    blk = lambda w: pl.BlockSpec((bb, rows, w), lambda bi, ci: (bi, ci, 0))
    per_b = lambda shape: pl.BlockSpec((bb,) + shape, lambda bi, ci: (bi,) + (0,) * len(shape))
    const = lambda shape: pl.BlockSpec(shape, lambda bi, ci: (0,) * len(shape))
    kernel = functools.partial(_mlstm_kernel, heads=heads, dk=dk, dv=dv, rows=rows, chunk=chunk,
                               t_valid=t_valid, bb=bb)
    o, c, n, m = pl.pallas_call(
        kernel,
        grid=(b // bb, nc),
        in_specs=[blk(heads * dk), blk(heads * dk), v_spec(bb, rows, nc), og_spec(bb, rows, nc), blk(LANES),
                  const((1, LANES)), const((1, dv)), per_b((heads, dk, dv)), per_b((heads, dk)),
                  per_b((heads, LANES))],
        out_specs=[blk(heads * dv), per_b((heads, dk, dv)), per_b((heads, dk)), per_b((heads, LANES))],
        out_shape=[jax.ShapeDtypeStruct((b, t, heads * dv), BF16 if rows == M_CHUNK else F32),
                   jax.ShapeDtypeStruct((b, heads, dk, dv), F32),
                   jax.ShapeDtypeStruct((b, heads, dk), F32),
                   jax.ShapeDtypeStruct((b, heads, LANES), F32)],
        scratch_shapes=[pltpu.VMEM((bb, heads, dk, dv), F32), pltpu.VMEM((bb, heads, dk), F32),
                        pltpu.VMEM((bb, heads, LANES), F32)],
        compiler_params=_params(("arbitrary", "arbitrary")),
        name="mlstm_chunkwise",
    )(q, k, v_src, og_src, gates, bg, outnorm.reshape(1, dv), c0, n0, m0b)
    return o, c, n, m[:, :, 0]


def _outproj_kernel(a1_ref, a2_ref, b1_ref, b2_ref, wa_ref, wb_ref, x1_ref, x2_ref, o_ref, *, nb_first):
    def project(a_ref, b_ref, x_ref):
        o_ref[...] = x_ref[...] + (jnp.dot(a_ref[...], wa_ref[...], preferred_element_type=F32)
                                   + jnp.dot(b_ref[...], wb_ref[...], preferred_element_type=F32))

    i = pl.program_id(0)
    pl.when(i < nb_first)(lambda: project(a1_ref, b1_ref, x1_ref))
    pl.when(i >= nb_first)(lambda: project(a2_ref, b2_ref, x2_ref))


def _output_projection(a_groups, b_groups, w, x_groups):
    n1, wa = a_groups[0].shape
    n2 = a_groups[1].shape[0]
    wb = b_groups[0].shape[1]
    d = w.shape[1]
    tm = _row_tile(math.gcd(n1, n2), 512)
    nb_first = n1 // tm
    tn = min(1024, d)
    return pl.pallas_call(
        functools.partial(_outproj_kernel, nb_first=nb_first),
        grid=((n1 + n2) // tm, d // tn),
        in_specs=[
            *_two_group_specs(tm, wa, nb_first),
            *_two_group_specs(tm, wb, nb_first),
            pl.BlockSpec((wa, tn), lambda i, j: (0, j)),
            pl.BlockSpec((wb, tn), lambda i, j: (wa // wb, j)),
            *_two_group_specs(tm, tn, nb_first, n_col=d // tn),
        ],
        out_specs=pl.BlockSpec((tm, tn), lambda i, j: (i, j)),
        out_shape=jax.ShapeDtypeStruct((n1 + n2, d), F32),
        compiler_params=_params(("arbitrary", "arbitrary")),
        name="output_projection",
    )(*a_groups, *b_groups, w, w, *x_groups)


def _router_kernel(x_ref, g_ref, whi_ref, wlo_ref, b_ref, xf_ref, lg_ref):
    x = x_ref[...]
    ms = jnp.mean(x * x, axis=-1, keepdims=True)
    xn = (x * lax.rsqrt(ms + EPS)) * g_ref[...]
    hi = xn.astype(BF16)
    lo = (xn - hi.astype(F32)).astype(BF16)
    xf_ref[...] = xn
    lg_ref[...] = b_ref[...] + jnp.dot(hi, whi_ref[...], preferred_element_type=F32) \
        + (jnp.dot(hi, wlo_ref[...], preferred_element_type=F32)
           + jnp.dot(lo, whi_ref[...], preferred_element_type=F32))


def _ffn_norm_router(x, g, w_route, b_route):
    n, d = x.shape
    tm = _row_tile(n, 256)
    whi = w_route.astype(BF16)
    wlo = (w_route - whi.astype(F32)).astype(BF16)
    const = lambda shape: pl.BlockSpec(shape, lambda i: (0, 0))
    return pl.pallas_call(
        _router_kernel,
        grid=(n // tm,),
        in_specs=[pl.BlockSpec((tm, d), lambda i: (i, 0)), const((1, d)), const((d, LANES)),
                  const((d, LANES)), const((1, LANES))],
        out_specs=[pl.BlockSpec((tm, d), lambda i: (i, 0)), pl.BlockSpec((tm, LANES), lambda i: (i, 0))],
        out_shape=[jax.ShapeDtypeStruct((n, d), F32), jax.ShapeDtypeStruct((n, LANES), F32)],
        compiler_params=_params(("arbitrary",)),
        name="ffn_norm_router",
    )(x, g.reshape(1, d), whi, wlo, b_route.reshape(1, LANES))


MOE_ROWS = 512
MOE_FF_TILE = 512
MOE_OUT_TILE = 2048


def _row_copy(src_ref, src_row, dst_ref, dst_row, sem):
    return pltpu.make_async_copy(src_ref.at[pl.ds(src_row, 1)], dst_ref.at[pl.ds(dst_row, 1)], sem)


DMA_UNROLL = 8


def _gather_rows_kernel(nu_ref, ids_ref, src_ref, o_ref, buf_ref, sem, *, rows):
    def start(r, carry):
        _row_copy(src_ref, ids_ref[0, 0, r], buf_ref, r, sem).start()
        return carry

    def wait(r, carry):
        _row_copy(src_ref, 0, buf_ref, r, sem).wait()
        return carry

    @pl.when(pl.program_id(0) < nu_ref[0])
    def _():
        lax.fori_loop(0, rows, start, 0, unroll=DMA_UNROLL)
        lax.fori_loop(0, rows, wait, 0, unroll=DMA_UNROLL)
        o_ref[...] = buf_ref[...].astype(o_ref.dtype)

    @pl.when(pl.program_id(0) >= nu_ref[0])
    def _():
        o_ref[...] = jnp.zeros(o_ref.shape, o_ref.dtype)


def _gather_rows(src, ids, n_used, out_dtype):
    n_rows = ids.shape[0]
    d = src.shape[1]
    nblk = n_rows // MOE_ROWS
    return pl.pallas_call(
        functools.partial(_gather_rows_kernel, rows=MOE_ROWS),
        grid_spec=pltpu.PrefetchScalarGridSpec(
            num_scalar_prefetch=1,
            grid=(nblk,),
            in_specs=[pl.BlockSpec((1, 1, MOE_ROWS), lambda i, nu: (i, 0, 0), memory_space=pltpu.SMEM),
                      pl.BlockSpec(memory_space=pl.ANY)],
            out_specs=pl.BlockSpec((MOE_ROWS, d), lambda i, nu: (i, 0)),
            scratch_shapes=[pltpu.VMEM((MOE_ROWS, d), src.dtype), pltpu.SemaphoreType.DMA(())],
        ),
        out_shape=jax.ShapeDtypeStruct((n_rows, d), out_dtype),
        compiler_params=_params(("arbitrary",)),
        name="moe_gather_rows",
    )(n_used, ids.reshape(nblk, 1, MOE_ROWS), src)


COMBINE_ROWS = 128


def _combine_kernel(slots_ref, x_ref, yb_ref, oa_ref, ob_ref, buf_ref, sem, *, tc, nb_first):
    n_copies = TOP_K * tc

    def start(r, carry):
        _row_copy(yb_ref, slots_ref[0, 0, r], buf_ref, r, sem).start()
        return carry

    def wait(r, carry):
        _row_copy(yb_ref, 0, buf_ref, r, sem).wait()
        return carry

    lax.fori_loop(0, n_copies, start, 0, unroll=DMA_UNROLL)
    lax.fori_loop(0, n_copies, wait, 0, unroll=DMA_UNROLL)
    acc = buf_ref[0:tc, :]
    for k in range(1, TOP_K):
        acc = acc + buf_ref[k * tc:(k + 1) * tc, :]
    i = pl.program_id(0)

    @pl.when(i < nb_first)
    def _():
        oa_ref[...] = x_ref[...] + acc

    @pl.when(i >= nb_first)
    def _():
        ob_ref[...] = x_ref[...] + acc


def _combine(x, yb, slot2, n_first):
    n, d = x.shape
    tc = _row_tile(math.gcd(n_first, n - n_first), COMBINE_ROWS)
    nblk = n // tc
    nb_first = n_first // tc
    slots = slot2.reshape(nblk, tc, TOP_K).transpose(0, 2, 1).reshape(nblk, 1, TOP_K * tc)
    return pl.pallas_call(
        functools.partial(_combine_kernel, tc=tc, nb_first=nb_first),
        grid=(nblk,),
        in_specs=[pl.BlockSpec((1, 1, TOP_K * tc), lambda i: (i, 0, 0), memory_space=pltpu.SMEM),
                  pl.BlockSpec((tc, d), lambda i: (i, 0)),
                  pl.BlockSpec(memory_space=pl.ANY)],
        out_specs=[pl.BlockSpec((tc, d), lambda i: (jnp.minimum(i, nb_first - 1), 0)),
                   pl.BlockSpec((tc, d), lambda i: (jnp.maximum(i - nb_first, 0), 0))],
        out_shape=[jax.ShapeDtypeStruct((n_first, d), F32), jax.ShapeDtypeStruct((n - n_first, d), F32)],
        scratch_shapes=[pltpu.VMEM((TOP_K * tc, d), F32), pltpu.SemaphoreType.DMA(())],
        compiler_params=_params(("arbitrary",)),
        name="moe_combine",
    )(slots, x, yb)


def _weights_changed(t, ib_ref, is_ref, be_ref):
    prev = jnp.maximum(t - 1, 0)
    return (t == 0) | (be_ref[ib_ref[t]] != be_ref[ib_ref[prev]]) | (is_ref[t] != is_ref[prev])


def _moe_up_kernel(ib_ref, is_ref, be_ref, ni_ref, x_ref, wg_ref, wu_ref, h_ref, wgb_ref, wub_ref):
    t = pl.program_id(0)

    @pl.when(t >= ni_ref[0])
    def _():
        h_ref[...] = jnp.zeros(h_ref.shape, h_ref.dtype)

    @pl.when(t < ni_ref[0])
    def _():
        @pl.when(_weights_changed(t, ib_ref, is_ref, be_ref))
        def _():
            wgb_ref[...] = wg_ref[0].astype(BF16)
            wub_ref[...] = wu_ref[0].astype(BF16)

        x = x_ref[...]
        gt = jnp.dot(x, wgb_ref[...], preferred_element_type=F32)
        up = jnp.dot(x, wub_ref[...], preferred_element_type=F32)
        h_ref[...] = ((gt * jax.nn.sigmoid(gt)) * up).astype(h_ref.dtype)


def _moe_down_kernel(ib_ref, is_ref, be_ref, ni_ref, h_ref, wd_ref, rw_ref, o_ref, wdb_ref):
    t = pl.program_id(0)

    @pl.when(t >= ni_ref[0])
    def _():
        o_ref[...] = jnp.zeros(o_ref.shape, F32)

    @pl.when(t < ni_ref[0])
    def _():
        @pl.when(_weights_changed(t, ib_ref, is_ref, be_ref))
        def _():
            wdb_ref[...] = wd_ref[0].astype(BF16)

        o_ref[...] = jnp.dot(h_ref[...], wdb_ref[...], preferred_element_type=F32) * rw_ref[...]


def _item_tables(block_e, blocks, bstart, n_used, nblk, nsplit):
    b = jnp.arange(nblk, dtype=jnp.int32)
    s = jnp.arange(nsplit, dtype=jnp.int32)
    e = block_e
    pos_live = nsplit * bstart[e][:, None] + s[None, :] * blocks[e][:, None] + (b - bstart[e])[:, None]
    pos_idle = b[:, None] * nsplit + s[None, :]
    pos = jnp.where((b < n_used[0])[:, None], pos_live, pos_idle).reshape(-1)
    item_b = jnp.zeros((nblk * nsplit,), jnp.int32).at[pos].set(jnp.repeat(b, nsplit))
    item_s = jnp.zeros((nblk * nsplit,), jnp.int32).at[pos].set(jnp.tile(s, nblk))
    return item_b, item_s, (n_used * nsplit).astype(jnp.int32)


def _moe_ffn(xb, row_w, block_e, blocks, bstart, n_used, w_gate, w_up, w_down):
    rows, d = xb.shape
    ff = w_gate.shape[2]
    tf = min(MOE_FF_TILE, ff)
    tn = min(MOE_OUT_TILE, d)
    nblk = rows // MOE_ROWS

    def live(t, ni):
        return jnp.minimum(t, ni[0] - 1)

    def run(kernel, name, nsplit, in_specs, out_spec, out_shape, scratch, args):
        item_b, item_s, n_items = _item_tables(block_e, blocks, bstart, n_used, nblk, nsplit)
        return pl.pallas_call(
            kernel,
            grid_spec=pltpu.PrefetchScalarGridSpec(
                num_scalar_prefetch=4, grid=(nblk * nsplit,), in_specs=in_specs, out_specs=out_spec,
                scratch_shapes=scratch),
            out_shape=out_shape,
            compiler_params=_params(("arbitrary",)),
            name=name,
        )(item_b, item_s, block_e, n_items, *args)

    rows_of = lambda w: pl.BlockSpec((MOE_ROWS, w), lambda t, ib, s, be, ni: (ib[live(t, ni)], 0))
    w_cols = lambda k, w: pl.BlockSpec(
        (1, k, w), lambda t, ib, s, be, ni: (be[ib[live(t, ni)]], 0, s[live(t, ni)]))
    own = lambda w: pl.BlockSpec((MOE_ROWS, w), lambda t, ib, s, be, ni: (ib[t], s[t]))
    hidden = run(_moe_up_kernel, "moe_expert_up", ff // tf,
                 [rows_of(d), w_cols(d, tf), w_cols(d, tf)], own(tf),
                 jax.ShapeDtypeStruct((rows, ff), BF16),
                 [pltpu.VMEM((d, tf), BF16), pltpu.VMEM((d, tf), BF16)], (xb, w_gate, w_up))
    return run(_moe_down_kernel, "moe_expert_down", d // tn,
               [rows_of(ff), w_cols(ff, tn), rows_of(1)], own(tn),
               jax.ShapeDtypeStruct((rows, d), F32),
               [pltpu.VMEM((ff, tn), BF16)], (hidden, w_down, row_w))


def _route(logits, n_groups, n_experts):
    per = n_experts // n_groups
    n = logits.shape[0]
    lg = logits[:, :n_groups]
    le = logits[:, n_groups:n_groups + n_experts]
    gsm = jax.nn.softmax(lg, axis=-1)
    gidx = jnp.argmax(gsm, axis=-1)
    gval = jnp.max(gsm, axis=-1)
    le = jnp.take_along_axis(le.reshape(n, n_groups, per), gidx[:, None, None], axis=1)[:, 0]
    assert TOP_K == 2
    i1 = jnp.argmax(le, axis=-1)
    rest = jnp.where(jnp.arange(per)[None, :] == i1[:, None], -jnp.inf, le)
    i2 = jnp.argmax(rest, axis=-1)
    ev = jnp.stack([jnp.max(le, axis=-1), jnp.max(rest, axis=-1)], axis=-1)
    wts = gval[:, None] * jax.nn.softmax(ev, axis=-1)
    eid = (gidx[:, None] * per + jnp.stack([i1, i2], axis=-1)).astype(jnp.int32)
    return eid, wts


def _dispatch(eid, n_experts, n_rows):
    flat = eid.reshape(-1)
    onehot = (flat[:, None] == jnp.arange(n_experts, dtype=jnp.int32)[None, :]).astype(jnp.int32)
    rank = jnp.take_along_axis(jnp.cumsum(onehot, axis=0) - onehot, flat[:, None], axis=1)[:, 0]
    counts = jnp.sum(onehot, axis=0)
    blocks = (counts + MOE_ROWS - 1) // MOE_ROWS
    bend = jnp.cumsum(blocks)
    bstart = bend - blocks
    slot = bstart[flat] * MOE_ROWS + rank
    n_used = bend[-1:].astype(jnp.int32)
    nblk = n_rows // MOE_ROWS
    block_e = jnp.searchsorted(bend, jnp.arange(nblk, dtype=jnp.int32), side='right')
    block_e = jnp.clip(block_e, 0, n_experts - 1).astype(jnp.int32)
    return slot.astype(jnp.int32), block_e, blocks.astype(jnp.int32), bstart.astype(jnp.int32), n_used


def _moe(x1, n_first, norm_ffn, w_group, b_group, w_router, b_router, w_gate, w_up, w_down):
    n, d = x1.shape
    n_groups = w_group.shape[1]
    n_experts = w_router.shape[1]
    used = n_groups + n_experts
    w_route = jnp.pad(jnp.concatenate([w_group, w_router], axis=1), ((0, 0), (0, LANES - used)))
    b_route = jnp.pad(jnp.concatenate([b_group, b_router]), (0, LANES - used))
    xf, logits = _ffn_norm_router(x1, norm_ffn, w_route, b_route)
    eid, wts = _route(logits, n_groups, n_experts)
    n_assign = n * TOP_K
    n_rows = (n_assign // MOE_ROWS + n_experts) * MOE_ROWS
    slot, block_e, blocks, bstart, n_used = _dispatch(eid, n_experts, n_rows)
    row_a = jnp.full((n_rows,), -1, jnp.int32).at[slot].set(jnp.arange(n_assign, dtype=jnp.int32))
    row_tok = jnp.where(row_a >= 0, row_a // TOP_K, jnp.arange(n_rows, dtype=jnp.int32) % n)
    row_w = jnp.where(row_a >= 0, wts.reshape(-1)[jnp.maximum(row_a, 0)], 0.0)
    xb = _gather_rows(xf, row_tok, n_used, BF16)
    yb = _moe_ffn(xb, row_w.reshape(n_rows, 1), block_e, blocks, bstart, n_used, w_gate, w_up, w_down)
    return _combine(x1, yb, slot.reshape(n, TOP_K), n_first)


def _layer(xp, xs, conv_s, c_s, n_s, m_s, cache_k, cache_v, page_table, lambda_init, p):
    (norm_mix, w_in, b_gate, conv_w, conv_b, q_norm, k_norm, da_lambda, da_subln,
     m_outnorm, w_out, norm_ffn, w_group, b_group, w_router, b_router, w_gate, w_up, w_down) = p
    bp, tp, d = xp.shape
    bs, ts, _ = xs.shape
    n_p, n_s_rows = bp * tp, bs * ts
    heads_m, dk, dv = c_s.shape[1:]
    da_heads = cache_k.shape[2]
    da_w = da_heads * LANES
    m_w = heads_m * dv
    conv_c = conv_w.shape[1]
    assert da_w == m_w == conv_c, "the six projection regions are assumed equally wide"
    region = da_w
    n_reg = 6
    lp = da_lambda.astype(F32)
    lam = jnp.exp(jnp.sum(lp[0] * lp[1])) - jnp.exp(jnp.sum(lp[2] * lp[3])) + lambda_init
    out_scale = 1.0 - lambda_init

    xp2, xs2 = xp.reshape(n_p, d), xs.reshape(n_s_rows, d)
    w_g = jnp.pad(w_in[:, n_reg * region:], ((0, 0), (0, LANES - 2 * heads_m))).astype(BF16)
    proj, gates = _input_projection(xp2, xs2, norm_mix, w_in.astype(BF16), w_g, region, n_reg)
    proj_s = proj[:, n_p:]

    k_f32, q2, k2, v2 = _qk_prep(proj, q_norm, k_norm, tp)
    o_da_p = _prompt_attention(q2, k2, v2, bp, tp, lam, da_subln, out_scale)
    sample3 = lambda a: a[n_p:].reshape(bs, ts, 2 * da_w)
    o_da_s = _paged_attention(sample3(q2), sample3(k2), sample3(v2), cache_k, cache_v, page_table,
                              lam, da_subln, out_scale)

    q_scale = dk ** -0.5
    zeros_state = jnp.zeros((bp, SUBLANES, conv_c), F32)
    mq_p, mk_p = _conv_silu(proj, 3, bp, tp, zeros_state, conv_w, conv_b, q_scale)
    assert ts <= SUBLANES
    qk_s = proj_s[3].reshape(bs, ts, conv_c)
    xs16 = jnp.concatenate([jnp.zeros((bs, 2 * SUBLANES - ts - (CONV_W - 1), conv_c), F32), conv_s, qk_s], axis=1)
    xs16 = xs16.reshape(1, bs * 2 * SUBLANES, conv_c)
    mq_s, mk_s = _conv_silu(xs16, 0, bs, 2 * SUBLANES, jnp.zeros((bs, SUBLANES, conv_c), F32),
                            conv_w, conv_b, q_scale)
    pad8 = lambda a: jnp.pad(a, ((0, 0), (0, SUBLANES - ts), (0, 0)))
    mq_s = pad8(mq_s[:, 2 * SUBLANES - ts:]).astype(F32)
    mk_s = pad8(mk_s[:, 2 * SUBLANES - ts:]).astype(F32)

    def region_spec(r, nb_rows):
        def spec(bb, rows, nc):
            assert bb == 1
            return pl.BlockSpec((1, rows, region), lambda bi, ci: (r, bi * nc + ci, 0))
        return spec

    zc = jnp.zeros((bp, heads_m, dk, dv), F32)
    o_m_p, c_p, n_pp, m_p = _mlstm(mq_p, mk_p, proj, region_spec(4, n_p), proj, region_spec(5, n_p),
                                   gates[:n_p].reshape(bp, tp, LANES), b_gate, m_outnorm,
                                   zc, jnp.zeros((bp, heads_m, dk), F32), jnp.zeros((bp, heads_m), F32), M_CHUNK)
    seq_spec = lambda bb, rows, nc: pl.BlockSpec((bb, rows, region), lambda bi, ci: (bi, ci, 0))
    v_s8 = pad8(proj_s[4].reshape(bs, ts, region))
    og_s8 = pad8(proj_s[5].reshape(bs, ts, region))
    g_s8 = pad8(gates[n_p:].reshape(bs, ts, LANES))
    o_m_s, c_new, n_new, m_new = _mlstm(mq_s, mk_s, v_s8, seq_spec, og_s8, seq_spec, g_s8, b_gate, m_outnorm,
                                        c_s, n_s, m_s, ts)

    o_da = (o_da_p, o_da_s.reshape(n_s_rows, da_w).astype(BF16))
    o_m = (o_m_p.reshape(n_p, m_w), o_m_s[:, :ts].reshape(n_s_rows, m_w).astype(BF16))
    x1 = _output_projection(o_da, o_m, w_out.astype(BF16), (xp2, xs2))
    y_p, y_s = _moe(x1, n_p, norm_ffn, w_group, b_group, w_router, b_router, w_gate, w_up, w_down)

    k4 = lambda a, b_, t_: a.reshape(b_, t_, da_heads, LANES)
    conv_tail = lambda a, b_, t_: jnp.stack([a[(bi + 1) * t_ - (CONV_W - 1):(bi + 1) * t_] for bi in range(b_)])
    conv_new_s = jnp.concatenate([conv_s, qk_s], axis=1)[:, ts:]
    outs_p = (k4(k_f32[:n_p], bp, tp), k4(proj[2, :n_p], bp, tp), c_p, n_pp, m_p, conv_tail(proj[3], bp, tp))
    outs_s = (k4(k_f32[n_p:], bs, ts), k4(proj_s[2], bs, ts), c_new, n_new, m_new, conv_new_s)
    return y_p.reshape(bp, tp, d), y_s.reshape(bs, ts, d), outs_p, outs_s


def kernel(x_prompt, x_sample, cache_k, cache_v, state_C, state_n, state_m, state_conv, page_table, norm_mix, w_in, b_gate, conv_w, conv_b, q_norm, k_norm, da_lambda, da_subln, m_outnorm, w_out, norm_ffn, w_group, b_group, w_router, b_router, w_gate, w_up, w_down):
    depth = w_in.shape[0]
    yp, ys = x_prompt, x_sample
    outs = [[] for _ in range(12)]
    for l in range(depth):
        lambda_init = 0.8 - 0.6 * math.exp(-0.3 * l)
        p = (norm_mix[l], w_in[l], b_gate[l], conv_w[l], conv_b[l], q_norm[l], k_norm[l], da_lambda[l],
             da_subln[l], m_outnorm[l], w_out[l], norm_ffn[l], w_group[l], b_group[l], w_router[l],
             b_router[l], w_gate[l], w_up[l], w_down[l])
        yp, ys, outs_p, outs_s = _layer(yp, ys, state_conv[l], state_C[l], state_n[l], state_m[l],
                                        cache_k[l], cache_v[l], page_table, lambda_init, p)
        for i, a in enumerate(outs_p + outs_s):
            outs[i].append(a)
    return (yp, ys) + tuple(jnp.stack(o) for o in outs)
```
